```python
import math
import jax, jax.numpy as jnp
from jax import lax
import numpy as np

D_MODEL = 1024
BATCH = 8
SEQ = 4096
DEPTH = 4

CHUNK = 64
SSM_WIDTH = D_MODEL // 4
SSM_GROUP = 16
SSM_GROUPS = SSM_WIDTH // SSM_GROUP
SSM_STATE = 64
GMLP_WIDTH = D_MODEL // 4
GMLP_HEADS = 4
GMLP_HEAD_DIM = GMLP_WIDTH // GMLP_HEADS
GMLP_WINDOW = 128
DIFF_WIDTH = D_MODEL // 2
DIFF_HEADS = 4
DIFF_VDIM = DIFF_WIDTH // DIFF_HEADS
DIFF_QK_DIM = DIFF_VDIM // 2
MIX_WIDTH = SSM_WIDTH + GMLP_WIDTH + DIFF_WIDTH
IN_COLS = SSM_WIDTH + 2 * GMLP_WIDTH + 3 * DIFF_WIDTH
D_FF = ((8 * D_MODEL // 3 + 127) // 128) * 128
CONV_WIDTH = 3
ROPE_THETA = 10000.0
Q_BLOCK = 128
EPS = 1e-6

kernel_name = "hybrid_s5_sgu_diffattn_trunk"


def rms_norm(x, g):
    xf = x.astype(jnp.float32)
    y = xf * lax.rsqrt(jnp.mean(xf * xf, axis=-1, keepdims=True) + EPS)
    return (y * g.astype(jnp.float32)).astype(x.dtype)


def rope_tables(seq):
    half = DIFF_QK_DIM // 2
    inv = ROPE_THETA ** (-jnp.arange(half, dtype=jnp.float32) / half)
    ang = jnp.arange(seq, dtype=jnp.float32)[:, None] * inv[None, :]
    ang = jnp.concatenate([ang, ang], axis=-1)
    return jnp.cos(ang), jnp.sin(ang)


def apply_rope(x, cos, sin):
    x1, x2 = jnp.split(x, 2, axis=-1)
    rot = jnp.concatenate([-x2, x1], axis=-1)
    c = cos[None, :, None, None, :]
    s = sin[None, :, None, None, :]
    return (x.astype(jnp.float32) * c + rot.astype(jnp.float32) * s).astype(x.dtype)


def s5_mixer(u, a_re, a_im, log_dt, b_re, b_im, c_re, c_im, d_skip, w_glu, b_glu):
    bsz, seq, _ = u.shape
    uf = u.astype(jnp.float32).reshape(bsz, seq, SSM_GROUPS, SSM_GROUP)
    lam = lax.complex(a_re.astype(jnp.float32), a_im.astype(jnp.float32))
    dt = jnp.exp(log_dt.astype(jnp.float32))[:, None]
    a_bar = jnp.exp(lam * dt)
    b_mat = lax.complex(b_re.astype(jnp.float32), b_im.astype(jnp.float32))
    b_bar = ((a_bar - 1.0) / lam)[..., None] * b_mat
    bu = jnp.einsum("bsgh,gph->bsgp", uf.astype(jnp.complex64), b_bar)

    def combine(left, right):
        a_l, b_l = left
        a_r, b_r = right
        return a_l * a_r, a_r * b_l + b_r

    a_seq = jnp.broadcast_to(a_bar, (1, seq) + a_bar.shape)
    _, states = lax.associative_scan(combine, (a_seq, bu), axis=1)
    c_mat = lax.complex(c_re.astype(jnp.float32), c_im.astype(jnp.float32))
    y = jnp.real(jnp.einsum("bsgp,ghp->bsgh", states, c_mat))
    y = y + d_skip.astype(jnp.float32).reshape(SSM_GROUPS, SSM_GROUP) * uf
    y = jax.nn.gelu(y.reshape(bsz, seq, SSM_WIDTH))
    y = y * jax.nn.sigmoid(y @ w_glu.astype(jnp.float32) + b_glu.astype(jnp.float32))
    return y.astype(u.dtype)


def spatial_gating(u, v, v_gain, w_s, b_s):
    bsz, seq, _ = u.shape
    v = rms_norm(v, v_gain)
    nwin = seq // GMLP_WINDOW
    v = v.reshape(bsz, nwin, GMLP_WINDOW, GMLP_HEADS, GMLP_HEAD_DIM)
    pos_chunk = jnp.arange(GMLP_WINDOW) // CHUNK
    mask = pos_chunk[None, :] <= pos_chunk[:, None]
    w = jnp.where(mask[None], w_s, 0).astype(v.dtype)
    mixed = jnp.einsum("hij,bnjhc->bnihc", w, v)
    mixed = mixed + b_s.astype(v.dtype).T[None, None, :, :, None]
    return u * mixed.reshape(bsz, seq, GMLP_WIDTH)


def diff_attention(q, k, v, q_gain, k_gain, lam_q1, lam_k1, lam_q2, lam_k2, sub_gain, lambda_init):
    bsz, seq, _ = q.shape
    q = q.reshape(bsz, seq, DIFF_HEADS, 2, DIFF_QK_DIM)
    k = k.reshape(bsz, seq, DIFF_HEADS, 2, DIFF_QK_DIM)
    v = v.reshape(bsz, seq, DIFF_HEADS, DIFF_VDIM)
    q = rms_norm(q, q_gain)
    k = rms_norm(k, k_gain)
    cos, sin = rope_tables(seq)
    q = apply_rope(q, cos, sin)
    k = apply_rope(k, cos, sin)
    lam = (jnp.exp(jnp.sum(lam_q1.astype(jnp.float32) * lam_k1.astype(jnp.float32)))
           - jnp.exp(jnp.sum(lam_q2.astype(jnp.float32) * lam_k2.astype(jnp.float32)))
           + lambda_init)
    nblk = seq // Q_BLOCK
    q_blocks = jnp.moveaxis(q.reshape(bsz, nblk, Q_BLOCK, DIFF_HEADS, 2, DIFF_QK_DIM), 1, 0)
    key_chunk = jnp.arange(seq) // CHUNK
    scale = DIFF_QK_DIM ** -0.5

    def block(args):
        qb, blk = args
        q_chunk = (blk * Q_BLOCK + jnp.arange(Q_BLOCK)) // CHUNK
        mask = key_chunk[None, :] <= q_chunk[:, None]
        s = jnp.einsum("bqhcd,bkhcd->bhcqk", qb, k).astype(jnp.float32) * scale
        s = jnp.where(mask, s, -jnp.inf)
        p = jax.nn.softmax(s, axis=-1)
        w = (p[:, :, 0] - lam * p[:, :, 1]).astype(v.dtype)
        return jnp.einsum("bhqk,bkhe->bqhe", w, v)

    out = lax.map(block, (q_blocks, jnp.arange(nblk)))
    out = jnp.moveaxis(out, 0, 1).reshape(bsz, seq, DIFF_HEADS, DIFF_VDIM)
    out = rms_norm(out, sub_gain) * (1.0 - lambda_init)
    return out.reshape(bsz, seq, DIFF_WIDTH)


def conv_gated_mlp(x, w_up, conv_w, conv_b, w_down):
    seq = x.shape[1]
    h = x @ w_up
    hp = jnp.pad(h, ((0, 0), (CONV_WIDTH - 1, 0), (0, 0)))
    acc = conv_b
    for i in range(CONV_WIDTH):
        acc = acc + conv_w[i] * hp[:, i:i + seq]
    gate, val = jnp.split(acc, 2, axis=-1)
    return (jax.nn.gelu(gate) * val) @ w_down


def setup_inputs(seed: int = 0) -> dict:
    key = jax.random.key(seed)
    ks = iter(jax.random.split(key, 40))

    def nrm(shape, std):
        return jax.random.normal(next(ks), shape, jnp.float32) * std

    L, G, P, C = DEPTH, SSM_GROUPS, SSM_STATE, SSM_GROUP
    x = nrm((BATCH, SEQ, D_MODEL), 1.0)
    attn_norm_g = 1.0 + nrm((L, D_MODEL), 0.02)
    w_in = nrm((L, D_MODEL, IN_COLS), D_MODEL ** -0.5)
    ssm_a_re = -0.5 + nrm((L, G, P), 0.01)
    ssm_a_im = jnp.pi * jnp.arange(P, dtype=jnp.float32)[None, None, :] + nrm((L, G, P), 0.01)
    ssm_log_dt = jax.random.uniform(next(ks), (L, G), jnp.float32, math.log(1e-3), math.log(1e-1))
    ssm_b_re = nrm((L, G, P, C), (2 * C) ** -0.5)
    ssm_b_im = nrm((L, G, P, C), (2 * C) ** -0.5)
    ssm_c_re = nrm((L, G, C, P), P ** -0.5)
    ssm_c_im = nrm((L, G, C, P), P ** -0.5)
    ssm_d = nrm((L, SSM_WIDTH), 1.0)
    ssm_w_glu = nrm((L, SSM_WIDTH, SSM_WIDTH), SSM_WIDTH ** -0.5)
    ssm_b_glu = nrm((L, SSM_WIDTH), 0.02)
    gmlp_v_g = 1.0 + nrm((L, GMLP_WIDTH), 0.02)
    gmlp_w_s = nrm((L, GMLP_HEADS, GMLP_WINDOW, GMLP_WINDOW), GMLP_WINDOW ** -0.5)
    gmlp_b_s = 1.0 + nrm((L, GMLP_HEADS, GMLP_WINDOW), 0.02)
    q_norm_g = 1.0 + nrm((L, DIFF_QK_DIM), 0.02)
    k_norm_g = 1.0 + nrm((L, DIFF_QK_DIM), 0.02)
    lambda_q1 = nrm((L, DIFF_QK_DIM), 0.1)
    lambda_k1 = nrm((L, DIFF_QK_DIM), 0.1)
    lambda_q2 = nrm((L, DIFF_QK_DIM), 0.1)
    lambda_k2 = nrm((L, DIFF_QK_DIM), 0.1)
    subln_g = 1.0 + nrm((L, DIFF_VDIM), 0.02)
    w_out = nrm((L, MIX_WIDTH, D_MODEL), MIX_WIDTH ** -0.5)
    ffn_norm_g = 1.0 + nrm((L, D_MODEL), 0.02)
    w_up = nrm((L, D_MODEL, 2 * D_FF), D_MODEL ** -0.5)
    conv_w = nrm((L, CONV_WIDTH, 2 * D_FF), CONV_WIDTH ** -0.5)
    conv_b = nrm((L, 2 * D_FF), 0.02)
    w_down = nrm((L, D_FF, D_MODEL), D_FF ** -0.5)
    return {"x": x, "attn_norm_g": attn_norm_g, "w_in": w_in,
            "ssm_a_re": ssm_a_re, "ssm_a_im": ssm_a_im, "ssm_log_dt": ssm_log_dt,
            "ssm_b_re": ssm_b_re, "ssm_b_im": ssm_b_im, "ssm_c_re": ssm_c_re, "ssm_c_im": ssm_c_im,
            "ssm_d": ssm_d, "ssm_w_glu": ssm_w_glu, "ssm_b_glu": ssm_b_glu,
            "gmlp_v_g": gmlp_v_g, "gmlp_w_s": gmlp_w_s, "gmlp_b_s": gmlp_b_s,
            "q_norm_g": q_norm_g, "k_norm_g": k_norm_g,
            "lambda_q1": lambda_q1, "lambda_k1": lambda_k1, "lambda_q2": lambda_q2, "lambda_k2": lambda_k2,
            "subln_g": subln_g, "w_out": w_out, "ffn_norm_g": ffn_norm_g,
            "w_up": w_up, "conv_w": conv_w, "conv_b": conv_b, "w_down": w_down}


def reference(x, attn_norm_g, w_in, ssm_a_re, ssm_a_im, ssm_log_dt, ssm_b_re, ssm_b_im,
              ssm_c_re, ssm_c_im, ssm_d, ssm_w_glu, ssm_b_glu, gmlp_v_g, gmlp_w_s, gmlp_b_s,
              q_norm_g, k_norm_g, lambda_q1, lambda_k1, lambda_q2, lambda_k2, subln_g, w_out,
              ffn_norm_g, w_up, conv_w, conv_b, w_down):
    o1 = SSM_WIDTH
    o2 = o1 + GMLP_WIDTH
    o3 = o2 + GMLP_WIDTH
    o4 = o3 + DIFF_WIDTH
    o5 = o4 + DIFF_WIDTH
    h = x
    for layer in range(DEPTH):
        lambda_init = 0.8 - 0.6 * math.exp(-0.3 * layer)
        xn = rms_norm(h, attn_norm_g[layer])
        proj = xn @ w_in[layer]
        u_ssm, u_g, v_g, q, k, v = jnp.split(proj, [o1, o2, o3, o4, o5], axis=-1)
        y_ssm = s5_mixer(u_ssm, ssm_a_re[layer], ssm_a_im[layer], ssm_log_dt[layer],
                         ssm_b_re[layer], ssm_b_im[layer], ssm_c_re[layer], ssm_c_im[layer],
                         ssm_d[layer], ssm_w_glu[layer], ssm_b_glu[layer])
        y_sgu = spatial_gating(u_g, v_g, gmlp_v_g[layer], gmlp_w_s[layer], gmlp_b_s[layer])
        y_diff = diff_attention(q, k, v, q_norm_g[layer], k_norm_g[layer],
                                lambda_q1[layer], lambda_k1[layer], lambda_q2[layer], lambda_k2[layer],
                                subln_g[layer], lambda_init)
        mixed = jnp.concatenate([y_ssm, y_sgu, y_diff], axis=-1)
        h = h + mixed @ w_out[layer]
        xn = rms_norm(h, ffn_norm_g[layer])
        h = h + conv_gated_mlp(xn, w_up[layer], conv_w[layer], conv_b[layer], w_down[layer])
    return h
```

```python
import functools
import math

import jax
import jax.numpy as jnp
from jax import lax
from jax.experimental import pallas as pl
from jax.experimental.pallas import tpu as pltpu

BF16 = jnp.bfloat16
F32 = jnp.float32

EPS = 1e-6
ROPE_THETA = 10000.0
CHUNK = 64
LANES = 128
SUBLANES = 8
MASKED = -1e30
VMEM_LIMIT = 56 * 1024 * 1024

SSM_T = 16
SSM_GROUP_BLOCK = 4
TM_PROJ = 512
TM_FFN = 1024
TF_FFN = 256
TQ_ATTN = 512
TK_ATTN = 256


def _rms(x, g):
    return x * lax.rsqrt(jnp.mean(x * x, axis=-1, keepdims=True) + EPS) * g


def _params(*sem):
    return pltpu.CompilerParams(dimension_semantics=sem, vmem_limit_bytes=VMEM_LIMIT)


def _inproj_kernel(h_ref, g_ref, w_ref, cos_ref, sin_ref, qg_ref, kg_ref, ones_ref, o_ref,
                   *, o3, o4, o5, slab, head_dim):
    xn = _rms(h_ref[...], g_ref[...]).astype(BF16)
    o_ref[:, 0:o3] = jnp.dot(xn, w_ref[:, 0:o3], preferred_element_type=F32).astype(BF16)
    o_ref[:, o5:] = jnp.dot(xn, w_ref[:, o5:], preferred_element_type=F32).astype(BF16)
    cos = cos_ref[...]
    sin = sin_ref[...]
    lane = lax.broadcasted_iota(jnp.int32, cos.shape, 1)
    first_half = (lane & (head_dim - 1)) < head_dim // 2
    for start, stop, gain_ref in ((o3, o4, qg_ref), (o4, o5, kg_ref)):
        for c in range(start, stop, slab):
            y = jnp.dot(xn, w_ref[:, c:c + slab], preferred_element_type=F32)
            ss = jnp.dot((y * y).astype(BF16), ones_ref[...], preferred_element_type=F32)
            y = y * lax.rsqrt(ss * (1.0 / head_dim) + EPS) * gain_ref[...]
            rot = jnp.where(first_half,
                            pltpu.roll(y, slab - head_dim // 2, 1),
                            pltpu.roll(y, head_dim // 2, 1))
            o_ref[:, c:c + slab] = (y * cos + rot * sin).astype(BF16)


def _inproj(h, g, w, cos_t, sin_t, qg, kg, ones, *, seq, o3, o4, o5, head_dim):
    n, d = h.shape
    cols = w.shape[1]
    tm = TM_PROJ
    slab = cos_t.shape[1]
    tiles_per_seq = seq // tm
    kern = functools.partial(_inproj_kernel, o3=o3, o4=o4, o5=o5, slab=slab, head_dim=head_dim)
    return pl.pallas_call(
        kern,
        out_shape=jax.ShapeDtypeStruct((n, cols), BF16),
        grid=(n // tm,),
        in_specs=[
            pl.BlockSpec((tm, d), lambda i: (i, 0)),
            pl.BlockSpec((1, d), lambda i: (0, 0)),
            pl.BlockSpec((d, cols), lambda i: (0, 0)),
            pl.BlockSpec((tm, slab), lambda i: (i % tiles_per_seq, 0)),
            pl.BlockSpec((tm, slab), lambda i: (i % tiles_per_seq, 0)),
            pl.BlockSpec((1, slab), lambda i: (0, 0)),
            pl.BlockSpec((1, slab), lambda i: (0, 0)),
            pl.BlockSpec((slab, slab), lambda i: (0, 0)),
        ],
        out_specs=pl.BlockSpec((tm, cols), lambda i: (i, 0)),
        compiler_params=_params("parallel"),
        name="inproj",
    )(h, g, w, cos_t, sin_t, qg, kg, ones)


def _s5_kernel(u_ref, kin_ref, bst_ref, cst_ref, ac_ref, y_ref, v_ref, xp_ref,
               *, gb, nsteps, bsz, sp):
    for g in range(gb):
        v_ref[g] = jnp.dot(u_ref[g], bst_ref[g], preferred_element_type=F32)
    aa = [jnp.broadcast_to(ac_ref[g, 0:1, :], (bsz, sp)) for g in range(gb)]
    ab = [jnp.broadcast_to(ac_ref[g, 1:2, :], (bsz, sp)) for g in range(gb)]

    def body(n, carry):
        r = pl.multiple_of(n * bsz, bsz)
        new = []
        for g in range(gb):
            x, xs = carry[g]
            xp_ref[g, pl.ds(r, bsz), :] = x
            vv = v_ref[g, pl.ds(r, bsz), :]
            new.append((aa[g] * x + ab[g] * xs + vv[:, :sp],
                        aa[g] * xs - ab[g] * x + vv[:, sp:]))
        return tuple(new)

    zero = jnp.zeros((bsz, sp), F32)
    lax.fori_loop(0, nsteps, body, tuple((zero, zero) for _ in range(gb)))
    for g in range(gb):
        y = jnp.dot(u_ref[g], kin_ref[g], preferred_element_type=F32)
        y = y + jnp.dot(xp_ref[g].astype(BF16), cst_ref[g], preferred_element_type=F32)
        y_ref[g] = y.astype(BF16)


def _s5(u_t, kin, bst, cst, ac, *, bsz):
    groups, nr, tc = u_t.shape
    sp = cst.shape[1]
    gb = SSM_GROUP_BLOCK
    kern = functools.partial(_s5_kernel, gb=gb, nsteps=nr // bsz, bsz=bsz, sp=sp)
    return pl.pallas_call(
        kern,
        out_shape=jax.ShapeDtypeStruct((groups, nr, tc), BF16),
        grid=(groups // gb,),
        in_specs=[
            pl.BlockSpec((gb, nr, tc), lambda i: (i, 0, 0)),
            pl.BlockSpec((gb, tc, tc), lambda i: (i, 0, 0)),
            pl.BlockSpec((gb, tc, 2 * sp), lambda i: (i, 0, 0)),
            pl.BlockSpec((gb, sp, tc), lambda i: (i, 0, 0)),
            pl.BlockSpec((gb, 2, sp), lambda i: (i, 0, 0)),
        ],
        out_specs=pl.BlockSpec((gb, nr, tc), lambda i: (i, 0, 0)),
        scratch_shapes=[pltpu.VMEM((gb, nr, 2 * sp), F32), pltpu.VMEM((gb, nr, sp), F32)],
        compiler_params=_params("parallel"),
        name="s5",
    )(u_t, kin, bst, cst, ac)


def _s5_operators(a_re, a_im, log_dt, b_re, b_im, c_re, c_im, t_chunk):
    hi = lax.Precision.HIGHEST
    nl, ng, p = a_re.shape
    c = b_re.shape[-1]
    lam = lax.complex(a_re, a_im)
    ldt = lam * jnp.exp(log_dt)[..., None]
    a_bar = jnp.exp(ldt)
    b_bar = ((a_bar - 1.0) / lam)[..., None] * lax.complex(b_re, b_im)
    c_mat = lax.complex(c_re, c_im)
    tau = jnp.arange(t_chunk + 1, dtype=F32)
    pw = jnp.exp(tau[:, None, None, None] * ldt[None])

    cp = c_mat[None] * pw[:, :, :, None, :]
    ktau = jnp.real(jnp.einsum("tlghp,lgpc->tlghc", cp[:t_chunk], b_bar, precision=hi))
    s_idx = jnp.arange(t_chunk)[:, None]
    t_idx = jnp.arange(t_chunk)[None, :]
    diff = t_idx - s_idx
    kin = jnp.where((diff >= 0)[:, :, None, None, None, None],
                    ktau[jnp.clip(diff, 0, t_chunk - 1)], 0.0)
    kin = kin.transpose(2, 3, 0, 5, 1, 4).reshape(nl, ng, t_chunk * c, t_chunk * c)

    bs = pw[:t_chunk][::-1][:, :, :, :, None] * b_bar[None]
    bs = bs.transpose(1, 2, 0, 4, 3).reshape(nl, ng, t_chunk * c, p)
    bst = jnp.concatenate([bs.real, bs.imag, bs.imag, bs.real], axis=-1)

    cs = cp[1:].transpose(1, 2, 4, 0, 3).reshape(nl, ng, p, t_chunk * c)
    cst = jnp.concatenate([cs.real, -cs.imag], axis=2)

    at = pw[t_chunk]
    ac = jnp.stack([jnp.concatenate([at.real, at.real], -1),
                    jnp.concatenate([-at.imag, at.imag], -1)], axis=2)
    return kin.astype(BF16), bst.astype(BF16), cst.astype(BF16), ac.astype(F32)


def _attn_kernel(lam_ref, q_ref, k_ref, v_ref, g_ref, o_ref, m_ref, l_ref, acc_ref,
                 *, tq, tk, half):
    qi = pl.program_id(2)
    q = q_ref[...]
    lane = lax.broadcasted_iota(jnp.int32, q.shape, 1)
    zero = jnp.zeros_like(q)
    qs = (jnp.where(lane < half, q, zero), jnp.where(lane >= half, q, zero))
    m_ref[...] = jnp.full(m_ref.shape, MASKED, F32)
    l_ref[...] = jnp.zeros(l_ref.shape, F32)
    acc_ref[...] = jnp.zeros(acc_ref.shape, F32)

    def block(r0, kstart, mask):
        kb = k_ref[pl.ds(kstart, tk), :]
        vb = v_ref[pl.ds(kstart, tk), :]
        for c in range(2):
            s = lax.dot_general(qs[c][r0:], kb, (((1,), (1,)), ((), ())),
                                preferred_element_type=F32)
            if mask is not None:
                s = jnp.where(mask, s, MASKED)
            m_prev = m_ref[c, r0:, :]
            m_new = jnp.maximum(m_prev, jnp.max(s, axis=1, keepdims=True))
            alpha = jnp.exp2(m_prev - m_new)
            p = jnp.exp2(s - pltpu.repeat(m_new, tk // LANES, axis=1))
            l_ref[c, r0:, :] = alpha * l_ref[c, r0:, :] + jnp.sum(p, axis=1, keepdims=True)
            acc_ref[c, r0:, :] = alpha * acc_ref[c, r0:, :] + jnp.dot(
                p.astype(BF16), vb, preferred_element_type=F32)
            m_ref[c, r0:, :] = m_new

    def body(j, carry):
        block(0, pl.multiple_of(j * tk, tk), None)
        return carry

    lax.fori_loop(0, qi * (tq // tk), body, 0)
    shift = CHUNK.bit_length() - 1
    for jj in range(tq // tk):
        r0 = jj * tk
        row = lax.broadcasted_iota(jnp.int32, (tq - r0, tk), 0) + r0
        col = lax.broadcasted_iota(jnp.int32, (tq - r0, tk), 1) + jj * tk
        mask = (col >> shift) <= (row >> shift)
        block(r0, pl.multiple_of(qi * tq + jj * tk, tk), mask)

    o = acc_ref[0] / l_ref[0] - lam_ref[0] * (acc_ref[1] / l_ref[1])
    o_ref[...] = _rms(o, g_ref[...]).astype(BF16)


def _attention(lam, proj, sub_g, *, batch, seq, heads, vdim, qcol, kcol, vcol):
    tq, tk = TQ_ATTN, TK_ATTN
    nq = seq // tq
    kern = functools.partial(_attn_kernel, tq=tq, tk=tk, half=vdim // 2)
    return pl.pallas_call(
        kern,
        out_shape=jax.ShapeDtypeStruct((batch * seq, heads * vdim), BF16),
        grid=(batch, heads, nq),
        in_specs=[
            pl.BlockSpec(memory_space=pltpu.SMEM),
            pl.BlockSpec((tq, vdim), lambda b, h, i: (b * nq + i, qcol + h)),
            pl.BlockSpec((seq, vdim), lambda b, h, i: (b, kcol + h)),
            pl.BlockSpec((seq, vdim), lambda b, h, i: (b, vcol + h)),
            pl.BlockSpec((1, vdim), lambda b, h, i: (0, 0)),
        ],
        out_specs=pl.BlockSpec((tq, vdim), lambda b, h, i: (b * nq + i, h)),
        scratch_shapes=[pltpu.VMEM((2, tq, LANES), F32), pltpu.VMEM((2, tq, LANES), F32),
                        pltpu.VMEM((2, tq, vdim), F32)],
        compiler_params=_params("parallel", "parallel", "arbitrary"),
        name="diffattn",
    )(lam, proj, proj, proj, sub_g)


def _outproj_kernel(p_ref, y_ref, a_ref, h_ref, d_ref, wglu_ref, bglu_ref, vgain_ref, ws_ref,
                    bs_ref, wo_ref, g2_ref, ho_ref, xn_ref, sg_ref,
                    *, w_ssm, w_gmlp, win, heads):
    p = p_ref[...]
    u = p[:, 0:w_ssm].astype(F32)
    ug = p[:, w_ssm:w_ssm + w_gmlp].astype(F32)
    vg = p[:, w_ssm + w_gmlp:].astype(F32)

    y = jax.nn.gelu(y_ref[...].astype(F32) + d_ref[...] * u)
    z = jnp.dot(y.astype(BF16), wglu_ref[...], preferred_element_type=F32) + bglu_ref[...]
    y_ssm = (y * jax.nn.sigmoid(z)).astype(BF16)

    v = _rms(vg, vgain_ref[...])
    head_of_lane = lax.broadcasted_iota(jnp.int32, (win, w_gmlp), 1) // (w_gmlp // heads)
    for w in range(p.shape[0] // win):
        vw = v[w * win:(w + 1) * win, :]
        mixed = bs_ref[...]
        for hh in range(heads):
            vh = jnp.where(head_of_lane == hh, vw, 0.0).astype(BF16)
            mixed = mixed + jnp.dot(ws_ref[hh], vh, preferred_element_type=F32)
        sg_ref[w * win:(w + 1) * win, :] = (ug[w * win:(w + 1) * win, :] * mixed).astype(BF16)

    out = jnp.dot(y_ssm, wo_ref[0:w_ssm, :], preferred_element_type=F32)
    out = out + jnp.dot(sg_ref[...], wo_ref[w_ssm:w_ssm + w_gmlp, :], preferred_element_type=F32)
    out = out + jnp.dot(a_ref[...], wo_ref[w_ssm + w_gmlp:, :], preferred_element_type=F32)
    h_new = h_ref[...] + out
    ho_ref[...] = h_new
    xn_ref[...] = _rms(h_new, g2_ref[...]).astype(BF16)


def _outproj(proj, y_pre, attn, h, d_skip, w_glu, b_glu, v_gain, w_s, b_s, w_out, g2,
             *, w_ssm, w_gmlp):
    n, d = h.shape
    tm = TM_PROJ
    heads, win, _ = w_s.shape
    w_diff = attn.shape[1]
    front = w_ssm + 2 * w_gmlp
    kern = functools.partial(_outproj_kernel, w_ssm=w_ssm, w_gmlp=w_gmlp, win=win, heads=heads)
    const2 = lambda i: (0, 0)
    return pl.pallas_call(
        kern,
        out_shape=(jax.ShapeDtypeStruct((n, d), F32), jax.ShapeDtypeStruct((n, d), BF16)),
        grid=(n // tm,),
        in_specs=[
            pl.BlockSpec((tm, front), lambda i: (i, 0)),
            pl.BlockSpec((tm, w_ssm), lambda i: (i, 0)),
            pl.BlockSpec((tm, w_diff), lambda i: (i, 0)),
            pl.BlockSpec((tm, d), lambda i: (i, 0)),
            pl.BlockSpec((1, w_ssm), const2),
            pl.BlockSpec((w_ssm, w_ssm), const2),
            pl.BlockSpec((1, w_ssm), const2),
            pl.BlockSpec((1, w_gmlp), const2),
            pl.BlockSpec((heads, win, win), lambda i: (0, 0, 0)),
            pl.BlockSpec((win, w_gmlp), const2),
            pl.BlockSpec((w_ssm + w_gmlp + w_diff, d), const2),
            pl.BlockSpec((1, d), const2),
        ],
        out_specs=(pl.BlockSpec((tm, d), lambda i: (i, 0)), pl.BlockSpec((tm, d), lambda i: (i, 0))),
        scratch_shapes=[pltpu.VMEM((tm, w_gmlp), BF16)],
        compiler_params=_params("parallel"),
        name="outproj",
    )(proj, y_pre, attn, h, d_skip, w_glu, b_glu, v_gain, w_s, b_s, w_out, g2)


def _ffn_kernel(x_ref, halo_ref, h_ref, wg_ref, wv_ref, cwg_ref, cwv_ref, cbg_ref, cbv_ref,
                wd_ref, o_ref, sg_ref, sv_ref, *, tm, tiles_per_seq, taps):
    i = pl.program_id(0)

    @pl.when(pl.program_id(1) == 0)
    def _():
        o_ref[...] = h_ref[...]

    halo = halo_ref[...]
    halo = jnp.where(i % tiles_per_seq == 0, jnp.zeros_like(halo), halo)
    x = x_ref[...]

    def conv(w_ref, s_ref, cw_ref, cb_ref):
        s_ref[0:SUBLANES, :] = jnp.dot(halo, w_ref[...], preferred_element_type=F32)
        s_ref[SUBLANES:, :] = jnp.dot(x, w_ref[...], preferred_element_type=F32)
        acc = cb_ref[...]
        for k in range(taps):
            acc = acc + cw_ref[k:k + 1, :] * s_ref[pl.ds(SUBLANES - (taps - 1) + k, tm), :]
        return acc

    gate = conv(wg_ref, sg_ref, cwg_ref, cbg_ref)
    val = conv(wv_ref, sv_ref, cwv_ref, cbv_ref)
    a = (jax.nn.gelu(gate) * val).astype(BF16)
    o_ref[...] += jnp.dot(a, wd_ref[...], preferred_element_type=F32)


def _ffn(xn, h, w_up, conv_w, conv_b, w_down, *, seq):
    n, d = h.shape
    f = w_down.shape[0]
    taps = conv_w.shape[0]
    tm, tf = TM_FFN, TF_FFN
    nf = f // tf
    tiles_per_seq = seq // tm
    kern = functools.partial(_ffn_kernel, tm=tm, tiles_per_seq=tiles_per_seq, taps=taps)
    halo_blocks = tm // SUBLANES
    return pl.pallas_call(
        kern,
        out_shape=jax.ShapeDtypeStruct((n, d), F32),
        grid=(n // tm, nf),
        in_specs=[
            pl.BlockSpec((tm, d), lambda i, j: (i, 0)),
            pl.BlockSpec((SUBLANES, d), lambda i, j: (jnp.maximum(i * halo_blocks - 1, 0), 0)),
            pl.BlockSpec((tm, d), lambda i, j: (i, 0)),
            pl.BlockSpec((d, tf), lambda i, j: (0, j)),
            pl.BlockSpec((d, tf), lambda i, j: (0, nf + j)),
            pl.BlockSpec((taps, tf), lambda i, j: (0, j)),
            pl.BlockSpec((taps, tf), lambda i, j: (0, nf + j)),
            pl.BlockSpec((1, tf), lambda i, j: (0, j)),
            pl.BlockSpec((1, tf), lambda i, j: (0, nf + j)),
            pl.BlockSpec((tf, d), lambda i, j: (j, 0)),
        ],
        out_specs=pl.BlockSpec((tm, d), lambda i, j: (i, 0)),
        scratch_shapes=[pltpu.VMEM((tm + SUBLANES, tf), F32), pltpu.VMEM((tm + SUBLANES, tf), F32)],
        compiler_params=_params("parallel", "arbitrary"),
        name="convffn",
    )(xn, xn, h, w_up, w_up, conv_w, conv_w, conv_b, conv_b, w_down)


def _rope_tables(seq, head_dim, reps):
    half = head_dim // 2
    inv = ROPE_THETA ** (-jnp.arange(half, dtype=F32) / half)
    ang = jnp.arange(seq, dtype=F32)[:, None] * inv[None, :]
    ang = jnp.concatenate([ang, ang], axis=-1)
    sign = jnp.concatenate([-jnp.ones((half,), F32), jnp.ones((half,), F32)])
    return jnp.tile(jnp.cos(ang), (1, reps)), jnp.tile(jnp.sin(ang) * sign, (1, reps))


def kernel(x, attn_norm_g, w_in, ssm_a_re, ssm_a_im, ssm_log_dt, ssm_b_re, ssm_b_im, ssm_c_re, ssm_c_im, ssm_d, ssm_w_glu, ssm_b_glu, gmlp_v_g, gmlp_w_s, gmlp_b_s, q_norm_g, k_norm_g, lambda_q1, lambda_k1, lambda_q2, lambda_k2, subln_g, w_out, ffn_norm_g, w_up, conv_w, conv_b, w_down):
    batch, seq, d = x.shape
    depth, _, in_cols = w_in.shape
    groups, _, group_c = ssm_b_re.shape[1], ssm_b_re.shape[2], ssm_b_re.shape[3]
    w_ssm = ssm_d.shape[1]
    w_gmlp = gmlp_v_g.shape[1]
    w_diff = (in_cols - w_ssm - 2 * w_gmlp) // 3
    qk_dim = q_norm_g.shape[1]
    vdim = subln_g.shape[1]
    heads = w_diff // vdim
    gheads, win = gmlp_w_s.shape[1], gmlp_w_s.shape[2]
    o3 = w_ssm + 2 * w_gmlp
    o4 = o3 + w_diff
    o5 = o4 + w_diff
    assert vdim == LANES and 2 * qk_dim == vdim and batch % SUBLANES == 0
    assert SSM_T * group_c == 2 * LANES and seq % SSM_T == 0

    slab = 2 * LANES
    cos_t, sin_t = _rope_tables(seq, qk_dim, slab // qk_dim)
    lane = jnp.arange(slab)
    ones = (lane[:, None] // qk_dim == lane[None, :] // qk_dim).astype(BF16)
    q_scale = qk_dim ** -0.5 * math.log2(math.e)
    kin, bst, cst, ac = _s5_operators(ssm_a_re, ssm_a_im, ssm_log_dt, ssm_b_re, ssm_b_im,
                                      ssm_c_re, ssm_c_im, SSM_T)
    pos_chunk = jnp.arange(win) // CHUNK
    sgu_mask = pos_chunk[None, :] <= pos_chunk[:, None]
    nchunk = seq // SSM_T

    h = x.reshape(batch * seq, d)
    for layer in range(depth):
        lambda_init = 0.8 - 0.6 * math.exp(-0.3 * layer)
        lam = (jnp.exp(jnp.sum(lambda_q1[layer] * lambda_k1[layer]))
               - jnp.exp(jnp.sum(lambda_q2[layer] * lambda_k2[layer])) + lambda_init)
        proj = _inproj(
            h, attn_norm_g[layer][None], w_in[layer].astype(BF16), cos_t, sin_t,
            jnp.tile(q_norm_g[layer] * q_scale, slab // qk_dim)[None],
            jnp.tile(k_norm_g[layer], slab // qk_dim)[None], ones,
            seq=seq, o3=o3, o4=o4, o5=o5, head_dim=qk_dim)

        u_t = proj[:, :w_ssm].reshape(batch, nchunk, SSM_T, groups, group_c)
        u_t = u_t.transpose(3, 1, 0, 2, 4).reshape(groups, nchunk * batch, SSM_T * group_c)
        y_t = _s5(u_t, kin[layer], bst[layer], cst[layer], ac[layer], bsz=batch)
        y_pre = y_t.reshape(groups, nchunk, batch, SSM_T, group_c)
        y_pre = y_pre.transpose(2, 1, 3, 0, 4).reshape(batch * seq, w_ssm)

        attn = _attention(
            lam.reshape(1).astype(F32), proj, (subln_g[layer] * (1.0 - lambda_init))[None],
            batch=batch, seq=seq, heads=heads, vdim=vdim,
            qcol=o3 // vdim, kcol=o4 // vdim, vcol=o5 // vdim)

        w_s = jnp.where(sgu_mask[None], gmlp_w_s[layer], 0.0).astype(BF16)
        b_s = jnp.repeat(gmlp_b_s[layer].T, w_gmlp // gheads, axis=1)
        h, xn = _outproj(
            proj, y_pre, attn, h, ssm_d[layer][None], ssm_w_glu[layer].astype(BF16),
            ssm_b_glu[layer][None], gmlp_v_g[layer][None], w_s, b_s,
            w_out[layer].astype(BF16), ffn_norm_g[layer][None], w_ssm=w_ssm, w_gmlp=w_gmlp)
        h = _ffn(xn, h, w_up[layer].astype(BF16), conv_w[layer], conv_b[layer][None],
                 w_down[layer].astype(BF16), seq=seq)
    return h.reshape(batch, seq, d)
```

```python
import functools
import math

import jax
import jax.numpy as jnp
from jax import lax
from jax.experimental import pallas as pl
from jax.experimental.pallas import tpu as pltpu

BF16 = jnp.bfloat16
F32 = jnp.float32

EPS = 1e-6
ROPE_THETA = 10000.0
CHUNK = 64
LANES = 128
SUBLANES = 8
MASKED = -1e30
VMEM_LIMIT = 56 * 1024 * 1024
MAX_UNSHIFTED_SCORE = 60.0

TS_SSM = 128
TM_PROJ = 512
TM_FFN = 1024
TF_FFN = 256
TQ_ATTN = 512
TK_ATTN = 256


def _rms(x, g):
    return x * lax.rsqrt(jnp.mean(x * x, axis=-1, keepdims=True) + EPS) * g


def _params(*sem):
    return pltpu.CompilerParams(dimension_semantics=sem, vmem_limit_bytes=VMEM_LIMIT)


def _inproj_kernel(h_ref, g_ref, w_ref, cos_ref, sin_ref, qg_ref, kg_ref, ones_ref,
                   o_ref, u_ref, *, o1, o3, o4, o5, slab, head_dim):
    xn = _rms(h_ref[...], g_ref[...]).astype(BF16)
    front = jnp.dot(xn, w_ref[:, 0:o3], preferred_element_type=F32).astype(BF16)
    o_ref[:, 0:o3] = front
    u_ref[...] = front[:, 0:o1]
    o_ref[:, o5:] = jnp.dot(xn, w_ref[:, o5:], preferred_element_type=F32).astype(BF16)
    cos = cos_ref[...]
    sin = sin_ref[...]
    lane = lax.broadcasted_iota(jnp.int32, cos.shape, 1)
    first_half = (lane & (head_dim - 1)) < head_dim // 2
    for start, stop, gain_ref in ((o3, o4, qg_ref), (o4, o5, kg_ref)):
        for c in range(start, stop, slab):
            y = jnp.dot(xn, w_ref[:, c:c + slab], preferred_element_type=F32)
            ss = jnp.dot((y * y).astype(BF16), ones_ref[...], preferred_element_type=F32)
            y = y * lax.rsqrt(ss * (1.0 / head_dim) + EPS) * gain_ref[...]
            rot = jnp.where(first_half,
                            pltpu.roll(y, slab - head_dim // 2, 1),
                            pltpu.roll(y, head_dim // 2, 1))
            o_ref[:, c:c + slab] = (y * cos + rot * sin).astype(BF16)


def _inproj(h, g, w, cos_t, sin_t, qg, kg, ones, *, batch, seq, o1, o3, o4, o5, head_dim):
    n, d = h.shape
    cols = w.shape[1]
    tm = TM_PROJ
    slab = cos_t.shape[1]
    tps = seq // tm
    kern = functools.partial(_inproj_kernel, o1=o1, o3=o3, o4=o4, o5=o5, slab=slab,
                             head_dim=head_dim)
    return pl.pallas_call(
        kern,
        out_shape=(jax.ShapeDtypeStruct((n, cols), BF16),
                   jax.ShapeDtypeStruct((seq, batch * o1), BF16)),
        grid=(n // tm,),
        in_specs=[
            pl.BlockSpec((tm, d), lambda i: (i, 0)),
            pl.BlockSpec((1, d), lambda i: (0, 0)),
            pl.BlockSpec((d, cols), lambda i: (0, 0)),
            pl.BlockSpec((tm, slab), lambda i: (i % tps, 0)),
            pl.BlockSpec((tm, slab), lambda i: (i % tps, 0)),
            pl.BlockSpec((1, slab), lambda i: (0, 0)),
            pl.BlockSpec((1, slab), lambda i: (0, 0)),
            pl.BlockSpec((slab, slab), lambda i: (0, 0)),
        ],
        out_specs=(pl.BlockSpec((tm, cols), lambda i: (i, 0)),
                   pl.BlockSpec((tm, o1), lambda i: (i % tps, i // tps))),
        compiler_params=_params("parallel"),
        name="inproj",
    )(h, g, w, cos_t, sin_t, qg, kg, ones)


def _s5_kernel(u_ref, bt_ref, ct_ref, a_ref, y_ref, z_ref, x_ref, *, ts, bsz, ns):
    @pl.when(pl.program_id(0) == 0)
    def _():
        x_ref[...] = jnp.zeros(x_ref.shape, F32)

    z_ref[...] = jnp.dot(u_ref[...], bt_ref[...], preferred_element_type=F32)
    ar = jnp.broadcast_to(a_ref[0:1, :], (bsz, ns))
    ai = jnp.broadcast_to(a_ref[1:2, :], (bsz, ns))

    def body(t, carry):
        xr, xi = carry
        r = pl.multiple_of(t * bsz, bsz)
        nr = ar * xr - ai * xi + z_ref[pl.ds(r, bsz), 0:ns]
        ni = ar * xi + ai * xr + z_ref[pl.ds(r, bsz), ns:]
        z_ref[pl.ds(r, bsz), 0:ns] = nr
        z_ref[pl.ds(r, bsz), ns:] = ni
        return nr, ni

    xr, xi = lax.fori_loop(0, ts, body, (x_ref[:, 0:ns], x_ref[:, ns:]), unroll=4)
    x_ref[:, 0:ns] = xr
    x_ref[:, ns:] = xi
    y_ref[...] = jnp.dot(z_ref[...].astype(BF16), ct_ref[...],
                         preferred_element_type=F32).astype(BF16)


def _s5(u_tb, bt, ct, a, *, bsz):
    rows, width = u_tb.shape
    ns = a.shape[1]
    ts = TS_SSM
    kern = functools.partial(_s5_kernel, ts=ts, bsz=bsz, ns=ns)
    return pl.pallas_call(
        kern,
        out_shape=jax.ShapeDtypeStruct((rows, width), BF16),
        grid=(rows // (ts * bsz),),
        in_specs=[
            pl.BlockSpec((ts * bsz, width), lambda i: (i, 0)),
            pl.BlockSpec((width, 2 * ns), lambda i: (0, 0)),
            pl.BlockSpec((2 * ns, width), lambda i: (0, 0)),
            pl.BlockSpec((2, ns), lambda i: (0, 0)),
        ],
        out_specs=pl.BlockSpec((ts * bsz, width), lambda i: (i, 0)),
        scratch_shapes=[pltpu.VMEM((ts * bsz, 2 * ns), F32), pltpu.VMEM((bsz, 2 * ns), F32)],
        compiler_params=_params("arbitrary"),
        name="s5",
    )(u_tb, bt, ct, a)


def _s5_operators(a_re, a_im, log_dt, b_re, b_im, c_re, c_im):
    nl, ng, p = a_re.shape
    c = b_re.shape[-1]
    dt = jnp.exp(log_dt)[..., None]
    mag = jnp.exp(a_re * dt)
    ar = mag * jnp.cos(a_im * dt)
    ai = mag * jnp.sin(a_im * dt)
    den = a_re * a_re + a_im * a_im
    kr = ((ar - 1.0) * a_re + ai * a_im) / den
    ki = (ai * a_re - (ar - 1.0) * a_im) / den
    bbr = kr[..., None] * b_re - ki[..., None] * b_im
    bbi = kr[..., None] * b_im + ki[..., None] * b_re
    eye = jnp.eye(ng, dtype=F32)

    def expand_in(b):
        return jnp.einsum("lgpc,gh->lgchp", b, eye).reshape(nl, ng * c, ng * p)

    def expand_out(m):
        return jnp.einsum("lgcp,gh->lgphc", m, eye).reshape(nl, ng * p, ng * c)

    bt = jnp.concatenate([expand_in(bbr), expand_in(bbi)], axis=-1)
    ct = jnp.concatenate([expand_out(c_re), -expand_out(c_im)], axis=1)
    a = jnp.stack([ar.reshape(nl, ng * p), ai.reshape(nl, ng * p)], axis=1)
    return bt.astype(BF16), ct.astype(BF16), a


def _split_maps(q, half):
    lane = lax.broadcasted_iota(jnp.int32, q.shape, 1)
    zero = jnp.zeros_like(q)
    return jnp.where(lane < half, q, zero), jnp.where(lane >= half, q, zero)


def _chunk_mask(rows, cols, r0, c0):
    shift = CHUNK.bit_length() - 1
    row = lax.broadcasted_iota(jnp.int32, (rows, cols), 0) + r0
    col = lax.broadcasted_iota(jnp.int32, (rows, cols), 1) + c0
    return (col >> shift) <= (row >> shift)


def _attn_finish(acc0, l0, acc1, l1, lam, g_ref, o_ref):
    o = acc0 / l0 - lam * (acc1 / l1)
    o_ref[...] = _rms(o, g_ref[...]).astype(BF16)


def _attn_kernel(lam_ref, q_ref, k_ref, v_ref, g_ref, o_ref, m_ref, l_ref, acc_ref,
                 *, tq, tk, half):
    qi = pl.program_id(2)
    qs = _split_maps(q_ref[...], half)
    m_ref[...] = jnp.full(m_ref.shape, MASKED, F32)
    l_ref[...] = jnp.zeros(l_ref.shape, F32)
    acc_ref[...] = jnp.zeros(acc_ref.shape, F32)

    def block(r0, kstart, mask):
        kb = k_ref[pl.ds(kstart, tk), :]
        vb = v_ref[pl.ds(kstart, tk), :]
        for c in range(2):
            s = lax.dot_general(qs[c][r0:], kb, (((1,), (1,)), ((), ())),
                                preferred_element_type=F32)
            if mask is not None:
                s = jnp.where(mask, s, MASKED)
            m_prev = m_ref[c, r0:, :]
            m_new = jnp.maximum(m_prev, jnp.max(s, axis=1, keepdims=True))
            alpha = jnp.exp2(m_prev - m_new)
            p = jnp.exp2(s - pltpu.repeat(m_new, tk // LANES, axis=1))
            l_ref[c, r0:, :] = alpha * l_ref[c, r0:, :] + jnp.sum(p, axis=1, keepdims=True)
            acc_ref[c, r0:, :] = alpha * acc_ref[c, r0:, :] + jnp.dot(
                p.astype(BF16), vb, preferred_element_type=F32)
            m_ref[c, r0:, :] = m_new

    def body(j, carry):
        block(0, pl.multiple_of(j * tk, tk), None)
        return carry

    lax.fori_loop(0, qi * (tq // tk), body, 0)
    for jj in range(tq // tk):
        r0 = jj * tk
        block(r0, pl.multiple_of(qi * tq + jj * tk, tk), _chunk_mask(tq - r0, tk, r0, jj * tk))
    _attn_finish(acc_ref[0], l_ref[0], acc_ref[1], l_ref[1], lam_ref[0], g_ref, o_ref)


def _attn_unshifted_kernel(lam_ref, q_ref, k_ref, v_ref, g_ref, o_ref, l_ref, acc_ref,
                           *, tq, half):
    qi = pl.program_id(2)
    qs = _split_maps(q_ref[...], half)
    l_ref[...] = jnp.zeros(l_ref.shape, F32)
    acc_ref[...] = jnp.zeros(acc_ref.shape, F32)

    def block(kstart, mask):
        kb = k_ref[pl.ds(kstart, tq), :]
        vb = v_ref[pl.ds(kstart, tq), :]
        for c in range(2):
            s = lax.dot_general(qs[c], kb, (((1,), (1,)), ((), ())), preferred_element_type=F32)
            p = jnp.exp2(s)
            if mask is not None:
                p = jnp.where(mask, p, 0.0)
            part = p[:, 0:LANES]
            for k in range(1, tq // LANES):
                part = part + p[:, k * LANES:(k + 1) * LANES]
            l_ref[c] += part
            acc_ref[c] += jnp.dot(p.astype(BF16), vb, preferred_element_type=F32)

    def body(j, carry):
        block(pl.multiple_of(j * tq, tq), None)
        return carry

    lax.fori_loop(0, qi, body, 0)
    block(pl.multiple_of(qi * tq, tq), _chunk_mask(tq, tq, 0, 0))
    l0 = jnp.sum(l_ref[0], axis=1, keepdims=True)
    l1 = jnp.sum(l_ref[1], axis=1, keepdims=True)
    _attn_finish(acc_ref[0], l0, acc_ref[1], l1, lam_ref[0], g_ref, o_ref)


def _attention(lam, proj, sub_g, *, unshifted, batch, seq, heads, vdim, qcol, kcol, vcol):
    tq, tk = TQ_ATTN, TK_ATTN
    nq = seq // tq
    acc = pltpu.VMEM((2, tq, vdim), F32)
    stat = pltpu.VMEM((2, tq, LANES), F32)
    if unshifted:
        kern = functools.partial(_attn_unshifted_kernel, tq=tq, half=vdim // 2)
        scratch = [stat, acc]
        name = "diffattn_unshifted"
    else:
        kern = functools.partial(_attn_kernel, tq=tq, tk=tk, half=vdim // 2)
        scratch = [stat, stat, acc]
        name = "diffattn"
    return pl.pallas_call(
        kern,
        out_shape=jax.ShapeDtypeStruct((batch * seq, heads * vdim), BF16),
        grid=(batch, heads, nq),
        in_specs=[
            pl.BlockSpec(memory_space=pltpu.SMEM),
            pl.BlockSpec((tq, vdim), lambda b, h, i: (b * nq + i, qcol + h)),
            pl.BlockSpec((seq, vdim), lambda b, h, i: (b, kcol + h)),
            pl.BlockSpec((seq, vdim), lambda b, h, i: (b, vcol + h)),
            pl.BlockSpec((1, vdim), lambda b, h, i: (0, 0)),
        ],
        out_specs=pl.BlockSpec((tq, vdim), lambda b, h, i: (b * nq + i, h)),
        scratch_shapes=scratch,
        compiler_params=_params("parallel", "parallel", "arbitrary"),
        name=name,
    )(lam, proj, proj, proj, sub_g)


def _outproj_kernel(p_ref, y_ref, a_ref, h_ref, d_ref, wglu_ref, bglu_ref, vgain_ref, ws_ref,
                    bs_ref, wo_ref, g2_ref, ho_ref, xn_ref, sg_ref,
                    *, w_ssm, w_gmlp, win, heads):
    p = p_ref[...]
    u = p[:, 0:w_ssm].astype(F32)
    ug = p[:, w_ssm:w_ssm + w_gmlp].astype(F32)
    vg = p[:, w_ssm + w_gmlp:].astype(F32)

    y = jax.nn.gelu(y_ref[...].astype(F32) + d_ref[...] * u)
    z = jnp.dot(y.astype(BF16), wglu_ref[...], preferred_element_type=F32) + bglu_ref[...]
    y_ssm = (y * jax.nn.sigmoid(z)).astype(BF16)

    v = _rms(vg, vgain_ref[...])
    head_shift = (w_gmlp // heads).bit_length() - 1
    head_of_lane = lax.broadcasted_iota(jnp.int32, (win, w_gmlp), 1) >> head_shift
    for w in range(p.shape[0] // win):
        vw = v[w * win:(w + 1) * win, :]
        mixed = bs_ref[...]
        for hh in range(heads):
            vh = jnp.where(head_of_lane == hh, vw, 0.0).astype(BF16)
            mixed = mixed + jnp.dot(ws_ref[hh], vh, preferred_element_type=F32)
        sg_ref[w * win:(w + 1) * win, :] = (ug[w * win:(w + 1) * win, :] * mixed).astype(BF16)

    out = jnp.dot(y_ssm, wo_ref[0:w_ssm, :], preferred_element_type=F32)
    out = out + jnp.dot(sg_ref[...], wo_ref[w_ssm:w_ssm + w_gmlp, :], preferred_element_type=F32)
    out = out + jnp.dot(a_ref[...], wo_ref[w_ssm + w_gmlp:, :], preferred_element_type=F32)
    h_new = h_ref[...] + out
    ho_ref[...] = h_new
    xn_ref[...] = _rms(h_new, g2_ref[...]).astype(BF16)


def _outproj(proj, y_tb, attn, h, d_skip, w_glu, b_glu, v_gain, w_s, b_s, w_out, g2,
             *, seq, w_ssm, w_gmlp):
    n, d = h.shape
    tm = TM_PROJ
    tps = seq // tm
    heads, win, _ = w_s.shape
    w_diff = attn.shape[1]
    front = w_ssm + 2 * w_gmlp
    assert (w_gmlp // heads) & (w_gmlp // heads - 1) == 0
    kern = functools.partial(_outproj_kernel, w_ssm=w_ssm, w_gmlp=w_gmlp, win=win, heads=heads)
    const2 = lambda i: (0, 0)
    return pl.pallas_call(
        kern,
        out_shape=(jax.ShapeDtypeStruct((n, d), F32), jax.ShapeDtypeStruct((n, d), BF16)),
        grid=(n // tm,),
        in_specs=[
            pl.BlockSpec((tm, front), lambda i: (i, 0)),
            pl.BlockSpec((tm, w_ssm), lambda i: (i % tps, i // tps)),
            pl.BlockSpec((tm, w_diff), lambda i: (i, 0)),
            pl.BlockSpec((tm, d), lambda i: (i, 0)),
            pl.BlockSpec((1, w_ssm), const2),
            pl.BlockSpec((w_ssm, w_ssm), const2),
            pl.BlockSpec((1, w_ssm), const2),
            pl.BlockSpec((1, w_gmlp), const2),
            pl.BlockSpec((heads, win, win), lambda i: (0, 0, 0)),
            pl.BlockSpec((win, w_gmlp), const2),
            pl.BlockSpec((w_ssm + w_gmlp + w_diff, d), const2),
            pl.BlockSpec((1, d), const2),
        ],
        out_specs=(pl.BlockSpec((tm, d), lambda i: (i, 0)), pl.BlockSpec((tm, d), lambda i: (i, 0))),
        scratch_shapes=[pltpu.VMEM((tm, w_gmlp), BF16)],
        compiler_params=_params("parallel"),
        name="outproj",
    )(proj, y_tb, attn, h, d_skip, w_glu, b_glu, v_gain, w_s, b_s, w_out, g2)


def _ffn_kernel(x_ref, halo_ref, h_ref, wg_ref, wv_ref, cwg_ref, cwv_ref, cbg_ref, cbv_ref,
                wd_ref, o_ref, sg_ref, sv_ref, *, tm, tiles_per_seq, taps):
    i = pl.program_id(0)

    @pl.when(pl.program_id(1) == 0)
    def _():
        o_ref[...] = h_ref[...]

    halo = halo_ref[...]
    halo = jnp.where(i % tiles_per_seq == 0, jnp.zeros_like(halo), halo)
    x = x_ref[...]

    def conv(w_ref, s_ref, cw_ref, cb_ref):
        s_ref[0:SUBLANES, :] = jnp.dot(halo, w_ref[...], preferred_element_type=F32)
        s_ref[SUBLANES:, :] = jnp.dot(x, w_ref[...], preferred_element_type=F32)
        acc = cb_ref[...]
        for k in range(taps):
            acc = acc + cw_ref[k:k + 1, :] * s_ref[pl.ds(SUBLANES - (taps - 1) + k, tm), :]
        return acc

    gate = conv(wg_ref, sg_ref, cwg_ref, cbg_ref)
    val = conv(wv_ref, sv_ref, cwv_ref, cbv_ref)
    a = (jax.nn.gelu(gate) * val).astype(BF16)
    o_ref[...] += jnp.dot(a, wd_ref[...], preferred_element_type=F32)


def _ffn(xn, h, w_up, conv_w, conv_b, w_down, *, seq):
    n, d = h.shape
    f = w_down.shape[0]
    taps = conv_w.shape[0]
    tm, tf = TM_FFN, TF_FFN
    nf = f // tf
    tiles_per_seq = seq // tm
    kern = functools.partial(_ffn_kernel, tm=tm, tiles_per_seq=tiles_per_seq, taps=taps)
    halo_blocks = tm // SUBLANES
    return pl.pallas_call(
        kern,
        out_shape=jax.ShapeDtypeStruct((n, d), F32),
        grid=(n // tm, nf),
        in_specs=[
            pl.BlockSpec((tm, d), lambda i, j: (i, 0)),
            pl.BlockSpec((SUBLANES, d), lambda i, j: (jnp.maximum(i * halo_blocks - 1, 0), 0)),
            pl.BlockSpec((tm, d), lambda i, j: (i, 0)),
            pl.BlockSpec((d, tf), lambda i, j: (0, j)),
            pl.BlockSpec((d, tf), lambda i, j: (0, nf + j)),
            pl.BlockSpec((taps, tf), lambda i, j: (0, j)),
            pl.BlockSpec((taps, tf), lambda i, j: (0, nf + j)),
            pl.BlockSpec((1, tf), lambda i, j: (0, j)),
            pl.BlockSpec((1, tf), lambda i, j: (0, nf + j)),
            pl.BlockSpec((tf, d), lambda i, j: (j, 0)),
        ],
        out_specs=pl.BlockSpec((tm, d), lambda i, j: (i, 0)),
        scratch_shapes=[pltpu.VMEM((tm + SUBLANES, tf), F32), pltpu.VMEM((tm + SUBLANES, tf), F32)],
        compiler_params=_params("parallel", "arbitrary"),
        name="convffn",
    )(xn, xn, h, w_up, w_up, conv_w, conv_w, conv_b, conv_b, w_down)


def _rope_tables(seq, head_dim, reps):
    half = head_dim // 2
    inv = ROPE_THETA ** (-jnp.arange(half, dtype=F32) / half)
    ang = jnp.arange(seq, dtype=F32)[:, None] * inv[None, :]
    ang = jnp.concatenate([ang, ang], axis=-1)
    sign = jnp.concatenate([-jnp.ones((half,), F32), jnp.ones((half,), F32)])
    return jnp.tile(jnp.cos(ang), (1, reps)), jnp.tile(jnp.sin(ang) * sign, (1, reps))


def kernel(x, attn_norm_g, w_in, ssm_a_re, ssm_a_im, ssm_log_dt, ssm_b_re, ssm_b_im, ssm_c_re, ssm_c_im, ssm_d, ssm_w_glu, ssm_b_glu, gmlp_v_g, gmlp_w_s, gmlp_b_s, q_norm_g, k_norm_g, lambda_q1, lambda_k1, lambda_q2, lambda_k2, subln_g, w_out, ffn_norm_g, w_up, conv_w, conv_b, w_down):
    batch, seq, d = x.shape
    depth, _, in_cols = w_in.shape
    w_ssm = ssm_d.shape[1]
    w_gmlp = gmlp_v_g.shape[1]
    w_diff = (in_cols - w_ssm - 2 * w_gmlp) // 3
    qk_dim = q_norm_g.shape[1]
    vdim = subln_g.shape[1]
    heads = w_diff // vdim
    gheads, win = gmlp_w_s.shape[1], gmlp_w_s.shape[2]
    o3 = w_ssm + 2 * w_gmlp
    o4 = o3 + w_diff
    o5 = o4 + w_diff
    assert vdim == LANES and 2 * qk_dim == vdim and batch == SUBLANES

    slab = 2 * LANES
    cos_t, sin_t = _rope_tables(seq, qk_dim, slab // qk_dim)
    lane = jnp.arange(slab)
    ones = (lane[:, None] // qk_dim == lane[None, :] // qk_dim).astype(BF16)
    q_scale = qk_dim ** -0.5 * math.log2(math.e)
    bt, ct, a_bar = _s5_operators(ssm_a_re, ssm_a_im, ssm_log_dt, ssm_b_re, ssm_b_im,
                                  ssm_c_re, ssm_c_im)
    pos_chunk = jnp.arange(win) // CHUNK
    sgu_mask = pos_chunk[None, :] <= pos_chunk[:, None]
    attn_args = dict(batch=batch, seq=seq, heads=heads, vdim=vdim,
                     qcol=o3 // vdim, kcol=o4 // vdim, vcol=o5 // vdim)

    h = x.reshape(batch * seq, d)
    for layer in range(depth):
        lambda_init = 0.8 - 0.6 * math.exp(-0.3 * layer)
        lam = (jnp.exp(jnp.sum(lambda_q1[layer] * lambda_k1[layer]))
               - jnp.exp(jnp.sum(lambda_q2[layer] * lambda_k2[layer])) + lambda_init)
        q_gain = q_norm_g[layer] * q_scale
        proj, u_tb = _inproj(
            h, attn_norm_g[layer][None], w_in[layer].astype(BF16), cos_t, sin_t,
            jnp.tile(q_gain, slab // qk_dim)[None],
            jnp.tile(k_norm_g[layer], slab // qk_dim)[None], ones,
            batch=batch, seq=seq, o1=w_ssm, o3=o3, o4=o4, o5=o5, head_dim=qk_dim)

        y_tb = _s5(u_tb.reshape(seq * batch, w_ssm), bt[layer], ct[layer], a_bar[layer],
                   bsz=batch).reshape(seq, batch * w_ssm)

        score_bound = qk_dim * jnp.max(jnp.abs(q_gain)) * jnp.max(jnp.abs(k_norm_g[layer]))
        attn = lax.cond(
            score_bound <= MAX_UNSHIFTED_SCORE,
            functools.partial(_attention, unshifted=True, **attn_args),
            functools.partial(_attention, unshifted=False, **attn_args),
            lam.reshape(1).astype(F32), proj, (subln_g[layer] * (1.0 - lambda_init))[None])

        w_s = jnp.where(sgu_mask[None], gmlp_w_s[layer], 0.0).astype(BF16)
        b_s = jnp.repeat(gmlp_b_s[layer].T, w_gmlp // gheads, axis=1)
        h, xn = _outproj(
            proj, y_tb, attn, h, ssm_d[layer][None], ssm_w_glu[layer].astype(BF16),
            ssm_b_glu[layer][None], gmlp_v_g[layer][None], w_s, b_s,
            w_out[layer].astype(BF16), ffn_norm_g[layer][None], seq=seq, w_ssm=w_ssm,
            w_gmlp=w_gmlp)
        h = _ffn(xn, h, w_up[layer].astype(BF16), conv_w[layer], conv_b[layer][None],
                 w_down[layer].astype(BF16), seq=seq)
    return h.reshape(batch, seq, d)
```

```python
import functools
import math

import jax
import jax.numpy as jnp
from jax import lax
from jax.experimental import pallas as pl
from jax.experimental.pallas import tpu as pltpu

BF16 = jnp.bfloat16
F32 = jnp.float32

EPS = 1e-6
ROPE_THETA = 10000.0
CHUNK = 64
LANES = 128
SUBLANES = 8
MASKED = -1e30
VMEM_LIMIT = 56 * 1024 * 1024
MAX_UNSHIFTED_SCORE = 60.0

TS_SSM = 128
TM_PROJ = 512
TM_FFN = 1024
TF_FFN = 256
RC_FFN = 256
TQ_ATTN = 512
TK_ATTN = 256


def _rms(x, g):
    return x * lax.rsqrt(jnp.mean(x * x, axis=-1, keepdims=True) + EPS) * g


def _params(*sem, flags=None):
    return pltpu.CompilerParams(dimension_semantics=sem, vmem_limit_bytes=VMEM_LIMIT, flags=flags)


def _inproj_kernel(h_ref, g_ref, w_ref, cos_ref, sin_ref, qg_ref, kg_ref, ones_ref,
                   o_ref, u_ref, *, o1, o3, o4, o5, slab, head_dim):
    xn = _rms(h_ref[...], g_ref[...]).astype(BF16)
    front = jnp.dot(xn, w_ref[:, 0:o3], preferred_element_type=F32).astype(BF16)
    o_ref[:, 0:o3] = front
    u_ref[...] = front[:, 0:o1]
    o_ref[:, o5:] = jnp.dot(xn, w_ref[:, o5:], preferred_element_type=F32).astype(BF16)
    cos = cos_ref[...]
    sin = sin_ref[...]
    lane = lax.broadcasted_iota(jnp.int32, cos.shape, 1)
    first_half = (lane & (head_dim - 1)) < head_dim // 2
    for start, stop, gain_ref in ((o3, o4, qg_ref), (o4, o5, kg_ref)):
        for c in range(start, stop, slab):
            y = jnp.dot(xn, w_ref[:, c:c + slab], preferred_element_type=F32)
            ss = jnp.dot((y * y).astype(BF16), ones_ref[...], preferred_element_type=F32)
            y = y * lax.rsqrt(ss * (1.0 / head_dim) + EPS) * gain_ref[...]
            rot = jnp.where(first_half,
                            pltpu.roll(y, slab - head_dim // 2, 1),
                            pltpu.roll(y, head_dim // 2, 1))
            o_ref[:, c:c + slab] = (y * cos + rot * sin).astype(BF16)


def _inproj(h, g, w, cos_t, sin_t, qg, kg, ones, *, batch, seq, o1, o3, o4, o5, head_dim):
    n, d = h.shape
    cols = w.shape[1]
    tm = TM_PROJ
    slab = cos_t.shape[1]
    tps = seq // tm
    kern = functools.partial(_inproj_kernel, o1=o1, o3=o3, o4=o4, o5=o5, slab=slab,
                             head_dim=head_dim)
    return pl.pallas_call(
        kern,
        out_shape=(jax.ShapeDtypeStruct((n, cols), BF16),
                   jax.ShapeDtypeStruct((seq, batch * o1), BF16)),
        grid=(n // tm,),
        in_specs=[
            pl.BlockSpec((tm, d), lambda i: (i, 0)),
            pl.BlockSpec((1, d), lambda i: (0, 0)),
            pl.BlockSpec((d, cols), lambda i: (0, 0)),
            pl.BlockSpec((tm, slab), lambda i: (i % tps, 0)),
            pl.BlockSpec((tm, slab), lambda i: (i % tps, 0)),
            pl.BlockSpec((1, slab), lambda i: (0, 0)),
            pl.BlockSpec((1, slab), lambda i: (0, 0)),
            pl.BlockSpec((slab, slab), lambda i: (0, 0)),
        ],
        out_specs=(pl.BlockSpec((tm, cols), lambda i: (i, 0)),
                   pl.BlockSpec((tm, o1), lambda i: (i % tps, i // tps))),
        compiler_params=_params("parallel"),
        name="inproj",
    )(h, g, w, cos_t, sin_t, qg, kg, ones)


def _s5_kernel(u_ref, bt_ref, ct_ref, a_ref, y_ref, z_ref, x_ref, *, ts, bsz, ns):
    @pl.when(pl.program_id(0) == 0)
    def _():
        x_ref[...] = jnp.zeros(x_ref.shape, F32)

    z_ref[...] = jnp.dot(u_ref[...], bt_ref[...], preferred_element_type=F32)
    ar = jnp.broadcast_to(a_ref[0:1, :], (bsz, ns))
    ai = jnp.broadcast_to(a_ref[1:2, :], (bsz, ns))

    def body(t, carry):
        xr, xi = carry
        r = pl.multiple_of(t * bsz, bsz)
        nr = ar * xr - ai * xi + z_ref[pl.ds(r, bsz), 0:ns]
        ni = ar * xi + ai * xr + z_ref[pl.ds(r, bsz), ns:]
        z_ref[pl.ds(r, bsz), 0:ns] = nr
        z_ref[pl.ds(r, bsz), ns:] = ni
        return nr, ni

    xr, xi = lax.fori_loop(0, ts, body, (x_ref[:, 0:ns], x_ref[:, ns:]), unroll=4)
    x_ref[:, 0:ns] = xr
    x_ref[:, ns:] = xi
    y_ref[...] = jnp.dot(z_ref[...].astype(BF16), ct_ref[...],
                         preferred_element_type=F32).astype(BF16)


def _s5(u_tb, bt, ct, a, *, bsz):
    rows, width = u_tb.shape
    ns = a.shape[1]
    ts = TS_SSM
    kern = functools.partial(_s5_kernel, ts=ts, bsz=bsz, ns=ns)
    return pl.pallas_call(
        kern,
        out_shape=jax.ShapeDtypeStruct((rows, width), BF16),
        grid=(rows // (ts * bsz),),
        in_specs=[
            pl.BlockSpec((ts * bsz, width), lambda i: (i, 0)),
            pl.BlockSpec((width, 2 * ns), lambda i: (0, 0)),
            pl.BlockSpec((2 * ns, width), lambda i: (0, 0)),
            pl.BlockSpec((2, ns), lambda i: (0, 0)),
        ],
        out_specs=pl.BlockSpec((ts * bsz, width), lambda i: (i, 0)),
        scratch_shapes=[pltpu.VMEM((ts * bsz, 2 * ns), F32), pltpu.VMEM((bsz, 2 * ns), F32)],
        compiler_params=_params("arbitrary"),
        name="s5",
    )(u_tb, bt, ct, a)


def _s5_operators(a_re, a_im, log_dt, b_re, b_im, c_re, c_im):
    nl, ng, p = a_re.shape
    c = b_re.shape[-1]
    dt = jnp.exp(log_dt)[..., None]
    mag = jnp.exp(a_re * dt)
    ar = mag * jnp.cos(a_im * dt)
    ai = mag * jnp.sin(a_im * dt)
    den = a_re * a_re + a_im * a_im
    kr = ((ar - 1.0) * a_re + ai * a_im) / den
    ki = (ai * a_re - (ar - 1.0) * a_im) / den
    bbr = kr[..., None] * b_re - ki[..., None] * b_im
    bbi = kr[..., None] * b_im + ki[..., None] * b_re
    eye = jnp.eye(ng, dtype=F32)

    def expand_in(b):
        return jnp.einsum("lgpc,gh->lgchp", b, eye).reshape(nl, ng * c, ng * p)

    def expand_out(m):
        return jnp.einsum("lgcp,gh->lgphc", m, eye).reshape(nl, ng * p, ng * c)

    bt = jnp.concatenate([expand_in(bbr), expand_in(bbi)], axis=-1)
    ct = jnp.concatenate([expand_out(c_re), -expand_out(c_im)], axis=1)
    a = jnp.stack([ar.reshape(nl, ng * p), ai.reshape(nl, ng * p)], axis=1)
    return bt.astype(BF16), ct.astype(BF16), a


def _split_maps(q, half):
    lane = lax.broadcasted_iota(jnp.int32, q.shape, 1)
    zero = jnp.zeros_like(q)
    return jnp.where(lane < half, q, zero), jnp.where(lane >= half, q, zero)


def _chunk_mask(rows, cols, r0, c0):
    shift = CHUNK.bit_length() - 1
    row = lax.broadcasted_iota(jnp.int32, (rows, cols), 0) + r0
    col = lax.broadcasted_iota(jnp.int32, (rows, cols), 1) + c0
    return (col >> shift) <= (row >> shift)


def _attn_finish(acc0, l0, acc1, l1, lam, g_ref, o_ref):
    o = acc0 / l0 - lam * (acc1 / l1)
    o_ref[...] = _rms(o, g_ref[...]).astype(BF16)


def _attn_kernel(lam_ref, q_ref, k_ref, v_ref, g_ref, o_ref, m_ref, l_ref, acc_ref,
                 *, tq, tk, half):
    qi = pl.program_id(2)
    qs = _split_maps(q_ref[...], half)
    m_ref[...] = jnp.full(m_ref.shape, MASKED, F32)
    l_ref[...] = jnp.zeros(l_ref.shape, F32)
    acc_ref[...] = jnp.zeros(acc_ref.shape, F32)

    def block(r0, kstart, mask):
        kb = k_ref[pl.ds(kstart, tk), :]
        vb = v_ref[pl.ds(kstart, tk), :]
        for c in range(2):
            s = lax.dot_general(qs[c][r0:], kb, (((1,), (1,)), ((), ())),
                                preferred_element_type=F32)
            if mask is not None:
                s = jnp.where(mask, s, MASKED)
            m_prev = m_ref[c, r0:, :]
            m_new = jnp.maximum(m_prev, jnp.max(s, axis=1, keepdims=True))
            alpha = jnp.exp2(m_prev - m_new)
            p = jnp.exp2(s - jnp.concatenate([m_new] * (tk // LANES), axis=1))
            l_ref[c, r0:, :] = alpha * l_ref[c, r0:, :] + jnp.sum(p, axis=1, keepdims=True)
            acc_ref[c, r0:, :] = alpha * acc_ref[c, r0:, :] + jnp.dot(
                p.astype(BF16), vb, preferred_element_type=F32)
            m_ref[c, r0:, :] = m_new

    def body(j, carry):
        block(0, pl.multiple_of(j * tk, tk), None)
        return carry

    lax.fori_loop(0, qi * (tq // tk), body, 0)
    for jj in range(tq // tk):
        r0 = jj * tk
        block(r0, pl.multiple_of(qi * tq + jj * tk, tk), _chunk_mask(tq - r0, tk, r0, jj * tk))
    _attn_finish(acc_ref[0], l_ref[0], acc_ref[1], l_ref[1], lam_ref[0], g_ref, o_ref)


def _attn_unshifted_kernel(lam_ref, q_ref, k_ref, v_ref, g_ref, o_ref, l_ref, acc_ref,
                           *, tq, half):
    qi = pl.program_id(2)
    qs = _split_maps(q_ref[...], half)
    l_ref[...] = jnp.zeros(l_ref.shape, F32)
    acc_ref[...] = jnp.zeros(acc_ref.shape, F32)

    def block(r0, rows, kstart, ksize, mask):
        kb = k_ref[pl.ds(kstart, ksize), :]
        vb = v_ref[pl.ds(kstart, ksize), :]
        for c in range(2):
            s = lax.dot_general(qs[c][r0:r0 + rows], kb, (((1,), (1,)), ((), ())),
                                preferred_element_type=F32)
            p = jnp.exp2(s)
            if mask is not None:
                p = jnp.where(mask, p, 0.0)
            part = p[:, 0:LANES]
            for k in range(1, ksize // LANES):
                part = part + p[:, k * LANES:(k + 1) * LANES]
            l_ref[c, r0:r0 + rows, :] += part
            acc_ref[c, r0:r0 + rows, :] += jnp.dot(p.astype(BF16), vb,
                                                   preferred_element_type=F32)

    def body(j, carry):
        block(0, tq, pl.multiple_of(j * tq, tq), tq, None)
        return carry

    lax.fori_loop(0, qi, body, 0)
    block(0, tq, pl.multiple_of(qi * tq, tq), tq, _chunk_mask(tq, tq, 0, 0))
    l0 = jnp.sum(l_ref[0], axis=1, keepdims=True)
    l1 = jnp.sum(l_ref[1], axis=1, keepdims=True)
    _attn_finish(acc_ref[0], l0, acc_ref[1], l1, lam_ref[0], g_ref, o_ref)


def _attention(lam, proj, sub_g, *, unshifted, batch, seq, heads, vdim, qcol, kcol, vcol):
    tq, tk = TQ_ATTN, TK_ATTN
    nq = seq // tq
    acc = pltpu.VMEM((2, tq, vdim), F32)
    stat = pltpu.VMEM((2, tq, LANES), F32)
    if unshifted:
        kern = functools.partial(_attn_unshifted_kernel, tq=tq, half=vdim // 2)
        scratch = [stat, acc]
        name = "diffattn_unshifted"
    else:
        kern = functools.partial(_attn_kernel, tq=tq, tk=tk, half=vdim // 2)
        scratch = [stat, stat, acc]
        name = "diffattn"
    return pl.pallas_call(
        kern,
        out_shape=jax.ShapeDtypeStruct((batch * seq, heads * vdim), BF16),
        grid=(batch, heads, nq),
        in_specs=[
            pl.BlockSpec(memory_space=pltpu.SMEM),
            pl.BlockSpec((tq, vdim), lambda b, h, i: (b * nq + i, qcol + h)),
            pl.BlockSpec((seq, vdim), lambda b, h, i: (b, kcol + h)),
            pl.BlockSpec((seq, vdim), lambda b, h, i: (b, vcol + h)),
            pl.BlockSpec((1, vdim), lambda b, h, i: (0, 0)),
        ],
        out_specs=pl.BlockSpec((tq, vdim), lambda b, h, i: (b * nq + i, h)),
        scratch_shapes=scratch,
        compiler_params=_params("parallel", "parallel", "arbitrary"),
        name=name,
    )(lam, proj, proj, proj, sub_g)


def _outproj_kernel(p_ref, y_ref, a_ref, h_ref, d_ref, wglu_ref, bglu_ref, vgain_ref, ws_ref,
                    bs_ref, wo_ref, g2_ref, ho_ref, xn_ref, sg_ref,
                    *, w_ssm, w_gmlp, win, heads):
    p = p_ref[...]
    u = p[:, 0:w_ssm].astype(F32)
    ug = p[:, w_ssm:w_ssm + w_gmlp].astype(F32)
    vg = p[:, w_ssm + w_gmlp:].astype(F32)

    y = jax.nn.gelu(y_ref[...].astype(F32) + d_ref[...] * u)
    z = jnp.dot(y.astype(BF16), wglu_ref[...], preferred_element_type=F32) + bglu_ref[...]
    y_ssm = (y * jax.nn.sigmoid(z)).astype(BF16)

    v = _rms(vg, vgain_ref[...])
    head_shift = (w_gmlp // heads).bit_length() - 1
    head_of_lane = lax.broadcasted_iota(jnp.int32, (win, w_gmlp), 1) >> head_shift
    for w in range(p.shape[0] // win):
        vw = v[w * win:(w + 1) * win, :]
        mixed = bs_ref[...]
        for hh in range(heads):
            vh = jnp.where(head_of_lane == hh, vw, 0.0).astype(BF16)
            mixed = mixed + jnp.dot(ws_ref[hh], vh, preferred_element_type=F32)
        sg_ref[w * win:(w + 1) * win, :] = (ug[w * win:(w + 1) * win, :] * mixed).astype(BF16)

    out = jnp.dot(y_ssm, wo_ref[0:w_ssm, :], preferred_element_type=F32)
    out = out + jnp.dot(sg_ref[...], wo_ref[w_ssm:w_ssm + w_gmlp, :], preferred_element_type=F32)
    out = out + jnp.dot(a_ref[...], wo_ref[w_ssm + w_gmlp:, :], preferred_element_type=F32)
    h_new = h_ref[...] + out
    ho_ref[...] = h_new
    xn_ref[...] = _rms(h_new, g2_ref[...]).astype(BF16)


def _outproj(proj, y_tb, attn, h, d_skip, w_glu, b_glu, v_gain, w_s, b_s, w_out, g2,
             *, seq, w_ssm, w_gmlp):
    n, d = h.shape
    tm = TM_PROJ
    tps = seq // tm
    heads, win, _ = w_s.shape
    w_diff = attn.shape[1]
    front = w_ssm + 2 * w_gmlp
    assert (w_gmlp // heads) & (w_gmlp // heads - 1) == 0
    kern = functools.partial(_outproj_kernel, w_ssm=w_ssm, w_gmlp=w_gmlp, win=win, heads=heads)
    const2 = lambda i: (0, 0)
    return pl.pallas_call(
        kern,
        out_shape=(jax.ShapeDtypeStruct((n, d), F32), jax.ShapeDtypeStruct((n, d), BF16)),
        grid=(n // tm,),
        in_specs=[
            pl.BlockSpec((tm, front), lambda i: (i, 0)),
            pl.BlockSpec((tm, w_ssm), lambda i: (i % tps, i // tps)),
            pl.BlockSpec((tm, w_diff), lambda i: (i, 0)),
            pl.BlockSpec((tm, d), lambda i: (i, 0)),
            pl.BlockSpec((1, w_ssm), const2),
            pl.BlockSpec((w_ssm, w_ssm), const2),
            pl.BlockSpec((1, w_ssm), const2),
            pl.BlockSpec((1, w_gmlp), const2),
            pl.BlockSpec((heads, win, win), lambda i: (0, 0, 0)),
            pl.BlockSpec((win, w_gmlp), const2),
            pl.BlockSpec((w_ssm + w_gmlp + w_diff, d), const2),
            pl.BlockSpec((1, d), const2),
        ],
        out_specs=(pl.BlockSpec((tm, d), lambda i: (i, 0)), pl.BlockSpec((tm, d), lambda i: (i, 0))),
        scratch_shapes=[pltpu.VMEM((tm, w_gmlp), BF16)],
        compiler_params=_params("parallel"),
        name="outproj",
    )(proj, y_tb, attn, h, d_skip, w_glu, b_glu, v_gain, w_s, b_s, w_out, g2)


def _ffn_kernel(x_ref, halo_ref, h_ref, wg_ref, wv_ref, cwg_ref, cwv_ref, cbg_ref, cbv_ref,
                wd_ref, o_ref, sg0_ref, sv0_ref, sg1_ref, sv1_ref,
                *, tm, rc, tiles_per_seq, taps, nf):
    i = pl.program_id(0)
    j = pl.program_id(1)
    slots = ((sg0_ref, sv0_ref), (sg1_ref, sv1_ref))

    def up(r):
        x = x_ref[r * rc:(r + 1) * rc, :]
        outs = []
        for w_ref in (wg_ref, wv_ref):
            head = None
            if r == 0:
                halo = halo_ref[...]
                halo = jnp.where(i % tiles_per_seq == 0, jnp.zeros_like(halo), halo)
                head = jnp.dot(halo, w_ref[...], preferred_element_type=F32)
            outs.append((head, jnp.dot(x, w_ref[...], preferred_element_type=F32)))
        return outs

    def stash(slot, r, outs):
        for s_ref, (head, body) in zip(slots[slot], outs):
            if head is not None:
                s_ref[0:SUBLANES, :] = head
            s_ref[SUBLANES + r * rc:SUBLANES + (r + 1) * rc, :] = body

    def gated(slot, r):
        def conv(s_ref, cw_ref, cb_ref):
            acc = cb_ref[...]
            for k in range(taps):
                rows = pl.ds(SUBLANES - (taps - 1) + k + r * rc, rc)
                acc = acc + cw_ref[k:k + 1, :] * s_ref[rows, :]
            return acc

        gate = conv(slots[slot][0], cwg_ref, cbg_ref)
        val = conv(slots[slot][1], cwv_ref, cbv_ref)
        return (jax.nn.gelu(gate) * val).astype(BF16)

    def down(r, a):
        o_ref[r * rc:(r + 1) * rc, :] += jnp.dot(a, wd_ref[...], preferred_element_type=F32)

    @pl.when(j == 0)
    def _():
        for r in range(tm // rc):
            o_ref[r * rc:(r + 1) * rc, :] = h_ref[r * rc:(r + 1) * rc, :]
            stash(0, r, up(r))

    for parity in (0, 1):
        @pl.when((j > 0) & (j < nf) & (j % 2 == parity))
        def _():
            for r in range(tm // rc):
                outs = up(r)
                a = gated(1 - parity, r)
                stash(parity, r, outs)
                down(r, a)

    @pl.when(j == nf)
    def _():
        for r in range(tm // rc):
            down(r, gated((nf - 1) % 2, r))


def _ffn(xn, h, w_up, conv_w, conv_b, w_down, *, seq):
    n, d = h.shape
    f = w_down.shape[0]
    taps = conv_w.shape[0]
    tm, tf = TM_FFN, TF_FFN
    nf = f // tf
    tiles_per_seq = seq // tm
    kern = functools.partial(_ffn_kernel, tm=tm, rc=RC_FFN, tiles_per_seq=tiles_per_seq,
                             taps=taps, nf=nf)
    halo_blocks = tm // SUBLANES
    up_tile = lambda j: jnp.minimum(j, nf - 1)
    down_tile = lambda j: jnp.maximum(j - 1, 0)
    return pl.pallas_call(
        kern,
        out_shape=jax.ShapeDtypeStruct((n, d), F32),
        grid=(n // tm, nf + 1),
        in_specs=[
            pl.BlockSpec((tm, d), lambda i, j: (i, 0)),
            pl.BlockSpec((SUBLANES, d), lambda i, j: (jnp.maximum(i * halo_blocks - 1, 0), 0)),
            pl.BlockSpec((tm, d), lambda i, j: (i, 0)),
            pl.BlockSpec((d, tf), lambda i, j: (0, up_tile(j))),
            pl.BlockSpec((d, tf), lambda i, j: (0, nf + up_tile(j))),
            pl.BlockSpec((taps, tf), lambda i, j: (0, down_tile(j))),
            pl.BlockSpec((taps, tf), lambda i, j: (0, nf + down_tile(j))),
            pl.BlockSpec((1, tf), lambda i, j: (0, down_tile(j))),
            pl.BlockSpec((1, tf), lambda i, j: (0, nf + down_tile(j))),
            pl.BlockSpec((tf, d), lambda i, j: (down_tile(j), 0)),
        ],
        out_specs=pl.BlockSpec((tm, d), lambda i, j: (i, 0)),
        scratch_shapes=[pltpu.VMEM((tm + SUBLANES, tf), F32) for _ in range(4)],
        compiler_params=_params("parallel", "arbitrary"),
        name="convffn",
    )(xn, xn, h, w_up, w_up, conv_w, conv_w, conv_b, conv_b, w_down)


def _rope_tables(seq, head_dim, reps):
    half = head_dim // 2
    inv = ROPE_THETA ** (-jnp.arange(half, dtype=F32) / half)
    ang = jnp.arange(seq, dtype=F32)[:, None] * inv[None, :]
    ang = jnp.concatenate([ang, ang], axis=-1)
    sign = jnp.concatenate([-jnp.ones((half,), F32), jnp.ones((half,), F32)])
    return jnp.tile(jnp.cos(ang), (1, reps)), jnp.tile(jnp.sin(ang) * sign, (1, reps))


def kernel(x, attn_norm_g, w_in, ssm_a_re, ssm_a_im, ssm_log_dt, ssm_b_re, ssm_b_im, ssm_c_re, ssm_c_im, ssm_d, ssm_w_glu, ssm_b_glu, gmlp_v_g, gmlp_w_s, gmlp_b_s, q_norm_g, k_norm_g, lambda_q1, lambda_k1, lambda_q2, lambda_k2, subln_g, w_out, ffn_norm_g, w_up, conv_w, conv_b, w_down):
    batch, seq, d = x.shape
    depth, _, in_cols = w_in.shape
    w_ssm = ssm_d.shape[1]
    w_gmlp = gmlp_v_g.shape[1]
    w_diff = (in_cols - w_ssm - 2 * w_gmlp) // 3
    qk_dim = q_norm_g.shape[1]
    vdim = subln_g.shape[1]
    heads = w_diff // vdim
    gheads, win = gmlp_w_s.shape[1], gmlp_w_s.shape[2]
    o3 = w_ssm + 2 * w_gmlp
    o4 = o3 + w_diff
    o5 = o4 + w_diff
    assert vdim == LANES and 2 * qk_dim == vdim and batch == SUBLANES

    slab = 2 * LANES
    cos_t, sin_t = _rope_tables(seq, qk_dim, slab // qk_dim)
    lane = jnp.arange(slab)
    ones = (lane[:, None] // qk_dim == lane[None, :] // qk_dim).astype(BF16)
    q_scale = qk_dim ** -0.5 * math.log2(math.e)
    bt, ct, a_bar = _s5_operators(ssm_a_re, ssm_a_im, ssm_log_dt, ssm_b_re, ssm_b_im,
                                  ssm_c_re, ssm_c_im)
    pos_chunk = jnp.arange(win) // CHUNK
    sgu_mask = pos_chunk[None, :] <= pos_chunk[:, None]
    attn_args = dict(batch=batch, seq=seq, heads=heads, vdim=vdim,
                     qcol=o3 // vdim, kcol=o4 // vdim, vcol=o5 // vdim)

    h = x.reshape(batch * seq, d)
    for layer in range(depth):
        lambda_init = 0.8 - 0.6 * math.exp(-0.3 * layer)
        lam = (jnp.exp(jnp.sum(lambda_q1[layer] * lambda_k1[layer]))
               - jnp.exp(jnp.sum(lambda_q2[layer] * lambda_k2[layer])) + lambda_init)
        q_gain = q_norm_g[layer] * q_scale
        proj, u_tb = _inproj(
            h, attn_norm_g[layer][None], w_in[layer].astype(BF16), cos_t, sin_t,
            jnp.tile(q_gain, slab // qk_dim)[None],
            jnp.tile(k_norm_g[layer], slab // qk_dim)[None], ones,
            batch=batch, seq=seq, o1=w_ssm, o3=o3, o4=o4, o5=o5, head_dim=qk_dim)

        y_tb = _s5(u_tb.reshape(seq * batch, w_ssm), bt[layer], ct[layer], a_bar[layer],
                   bsz=batch).reshape(seq, batch * w_ssm)

        score_bound = qk_dim * jnp.max(jnp.abs(q_gain)) * jnp.max(jnp.abs(k_norm_g[layer]))
        attn = lax.cond(
            score_bound <= MAX_UNSHIFTED_SCORE,
            functools.partial(_attention, unshifted=True, **attn_args),
            functools.partial(_attention, unshifted=False, **attn_args),
            lam.reshape(1).astype(F32), proj, (subln_g[layer] * (1.0 - lambda_init))[None])

        w_s = jnp.where(sgu_mask[None], gmlp_w_s[layer], 0.0).astype(BF16)
        b_s = jnp.repeat(gmlp_b_s[layer].T, w_gmlp // gheads, axis=1)
        h, xn = _outproj(
            proj, y_tb, attn, h, ssm_d[layer][None], ssm_w_glu[layer].astype(BF16),
            ssm_b_glu[layer][None], gmlp_v_g[layer][None], w_s, b_s,
            w_out[layer].astype(BF16), ffn_norm_g[layer][None], seq=seq, w_ssm=w_ssm,
            w_gmlp=w_gmlp)
        h = _ffn(xn, h, w_up[layer].astype(BF16), conv_w[layer], conv_b[layer][None],
                 w_down[layer].astype(BF16), seq=seq)
    return h.reshape(batch, seq, d)
```

```python
import functools
import math

import jax
import jax.numpy as jnp
from jax import lax
from jax.experimental import pallas as pl
from jax.experimental.pallas import tpu as pltpu

BF16 = jnp.bfloat16
F32 = jnp.float32

EPS = 1e-6
ROPE_THETA = 10000.0
CHUNK = 64
LANES = 128
SUBLANES = 8
MASKED = -1e30
VMEM_LIMIT = 56 * 1024 * 1024
MAX_UNSHIFTED_SCORE = 60.0

TS_SSM = 128
TM_PROJ = 512
TM_FFN = 1024
TF_FFN = 1408
GELU_C = math.sqrt(2.0 / math.pi)
GELU_K = 0.044715
TQ_ATTN = 512
TK_ATTN = 256
TQ_UNSHIFTED = 1024
TK_UNSHIFTED = 512


def _rms(x, g):
    return x * lax.rsqrt(jnp.mean(x * x, axis=-1, keepdims=True) + EPS) * g


def _params(*sem, flags=None):
    return pltpu.CompilerParams(dimension_semantics=sem, vmem_limit_bytes=VMEM_LIMIT, flags=flags)


def _inproj_kernel(h_ref, g_ref, w_ref, cos_ref, sin_ref, qg_ref, kg_ref, ones_ref,
                   o_ref, u_ref, *, o1, o3, o4, o5, slab, head_dim):
    xn = _rms(h_ref[...], g_ref[...]).astype(BF16)
    front = jnp.dot(xn, w_ref[:, 0:o3], preferred_element_type=F32).astype(BF16)
    o_ref[:, 0:o3] = front
    u_ref[...] = front[:, 0:o1]
    o_ref[:, o5:] = jnp.dot(xn, w_ref[:, o5:], preferred_element_type=F32).astype(BF16)
    cos = cos_ref[...]
    sin = sin_ref[...]
    lane = lax.broadcasted_iota(jnp.int32, cos.shape, 1)
    first_half = (lane & (head_dim - 1)) < head_dim // 2
    for start, stop, gain_ref in ((o3, o4, qg_ref), (o4, o5, kg_ref)):
        for c in range(start, stop, slab):
            y = jnp.dot(xn, w_ref[:, c:c + slab], preferred_element_type=F32)
            ss = jnp.dot((y * y).astype(BF16), ones_ref[...], preferred_element_type=F32)
            y = y * lax.rsqrt(ss * (1.0 / head_dim) + EPS) * gain_ref[...]
            rot = jnp.where(first_half,
                            pltpu.roll(y, slab - head_dim // 2, 1),
                            pltpu.roll(y, head_dim // 2, 1))
            o_ref[:, c:c + slab] = (y * cos + rot * sin).astype(BF16)


def _inproj(h, g, w, cos_t, sin_t, qg, kg, ones, *, batch, seq, o1, o3, o4, o5, head_dim):
    n, d = h.shape
    cols = w.shape[1]
    tm = TM_PROJ
    slab = cos_t.shape[1]
    tps = seq // tm
    kern = functools.partial(_inproj_kernel, o1=o1, o3=o3, o4=o4, o5=o5, slab=slab,
                             head_dim=head_dim)
    return pl.pallas_call(
        kern,
        out_shape=(jax.ShapeDtypeStruct((n, cols), BF16),
                   jax.ShapeDtypeStruct((seq, batch * o1), BF16)),
        grid=(n // tm,),
        in_specs=[
            pl.BlockSpec((tm, d), lambda i: (i, 0)),
            pl.BlockSpec((1, d), lambda i: (0, 0)),
            pl.BlockSpec((d, cols), lambda i: (0, 0)),
            pl.BlockSpec((tm, slab), lambda i: (i % tps, 0)),
            pl.BlockSpec((tm, slab), lambda i: (i % tps, 0)),
            pl.BlockSpec((1, slab), lambda i: (0, 0)),
            pl.BlockSpec((1, slab), lambda i: (0, 0)),
            pl.BlockSpec((slab, slab), lambda i: (0, 0)),
        ],
        out_specs=(pl.BlockSpec((tm, cols), lambda i: (i, 0)),
                   pl.BlockSpec((tm, o1), lambda i: (i % tps, i // tps))),
        compiler_params=_params("parallel"),
        name="inproj",
    )(h, g, w, cos_t, sin_t, qg, kg, ones)


def _s5_kernel(u_ref, bt_ref, ct_ref, a_ref, y_ref, z_ref, x_ref, *, ts, bsz, ns):
    @pl.when(pl.program_id(0) == 0)
    def _():
        x_ref[...] = jnp.zeros(x_ref.shape, F32)

    z_ref[...] = jnp.dot(u_ref[...], bt_ref[...], preferred_element_type=F32)
    ar = jnp.broadcast_to(a_ref[0:1, :], (bsz, ns))
    ai = jnp.broadcast_to(a_ref[1:2, :], (bsz, ns))

    def body(t, carry):
        xr, xi = carry
        r = pl.multiple_of(t * bsz, bsz)
        nr = ar * xr - ai * xi + z_ref[pl.ds(r, bsz), 0:ns]
        ni = ar * xi + ai * xr + z_ref[pl.ds(r, bsz), ns:]
        z_ref[pl.ds(r, bsz), 0:ns] = nr
        z_ref[pl.ds(r, bsz), ns:] = ni
        return nr, ni

    xr, xi = lax.fori_loop(0, ts, body, (x_ref[:, 0:ns], x_ref[:, ns:]), unroll=4)
    x_ref[:, 0:ns] = xr
    x_ref[:, ns:] = xi
    y_ref[...] = jnp.dot(z_ref[...].astype(BF16), ct_ref[...],
                         preferred_element_type=F32).astype(BF16)


def _s5(u_tb, bt, ct, a, *, bsz):
    rows, width = u_tb.shape
    ns = a.shape[1]
    ts = TS_SSM
    kern = functools.partial(_s5_kernel, ts=ts, bsz=bsz, ns=ns)
    return pl.pallas_call(
        kern,
        out_shape=jax.ShapeDtypeStruct((rows, width), BF16),
        grid=(rows // (ts * bsz),),
        in_specs=[
            pl.BlockSpec((ts * bsz, width), lambda i: (i, 0)),
            pl.BlockSpec((width, 2 * ns), lambda i: (0, 0)),
            pl.BlockSpec((2 * ns, width), lambda i: (0, 0)),
            pl.BlockSpec((2, ns), lambda i: (0, 0)),
        ],
        out_specs=pl.BlockSpec((ts * bsz, width), lambda i: (i, 0)),
        scratch_shapes=[pltpu.VMEM((ts * bsz, 2 * ns), F32), pltpu.VMEM((bsz, 2 * ns), F32)],
        compiler_params=_params("arbitrary"),
        name="s5",
    )(u_tb, bt, ct, a)


def _s5_operators(a_re, a_im, log_dt, b_re, b_im, c_re, c_im):
    nl, ng, p = a_re.shape
    c = b_re.shape[-1]
    dt = jnp.exp(log_dt)[..., None]
    mag = jnp.exp(a_re * dt)
    ar = mag * jnp.cos(a_im * dt)
    ai = mag * jnp.sin(a_im * dt)
    den = a_re * a_re + a_im * a_im
    kr = ((ar - 1.0) * a_re + ai * a_im) / den
    ki = (ai * a_re - (ar - 1.0) * a_im) / den
    bbr = kr[..., None] * b_re - ki[..., None] * b_im
    bbi = kr[..., None] * b_im + ki[..., None] * b_re
    eye = jnp.eye(ng, dtype=F32)

    def expand_in(b):
        return jnp.einsum("lgpc,gh->lgchp", b, eye).reshape(nl, ng * c, ng * p)

    def expand_out(m):
        return jnp.einsum("lgcp,gh->lgphc", m, eye).reshape(nl, ng * p, ng * c)

    bt = jnp.concatenate([expand_in(bbr), expand_in(bbi)], axis=-1)
    ct = jnp.concatenate([expand_out(c_re), -expand_out(c_im)], axis=1)
    a = jnp.stack([ar.reshape(nl, ng * p), ai.reshape(nl, ng * p)], axis=1)
    return bt.astype(BF16), ct.astype(BF16), a


def _split_maps(q, half):
    lane = lax.broadcasted_iota(jnp.int32, q.shape, 1)
    zero = jnp.zeros_like(q)
    return jnp.where(lane < half, q, zero), jnp.where(lane >= half, q, zero)


def _chunk_mask(rows, cols, r0, c0):
    shift = CHUNK.bit_length() - 1
    row = lax.broadcasted_iota(jnp.int32, (rows, cols), 0) + r0
    col = lax.broadcasted_iota(jnp.int32, (rows, cols), 1) + c0
    return (col >> shift) <= (row >> shift)


def _attn_finish(acc0, l0, acc1, l1, lam, g_ref, o_ref):
    o = acc0 / l0 - lam * (acc1 / l1)
    o_ref[...] = _rms(o, g_ref[...]).astype(BF16)


def _attn_kernel(lam_ref, q_ref, k_ref, v_ref, g_ref, o_ref, m_ref, l_ref, acc_ref,
                 *, tq, tk, half):
    qi = pl.program_id(2)
    qs = _split_maps(q_ref[...], half)
    m_ref[...] = jnp.full(m_ref.shape, MASKED, F32)
    l_ref[...] = jnp.zeros(l_ref.shape, F32)
    acc_ref[...] = jnp.zeros(acc_ref.shape, F32)

    def block(r0, kstart, mask):
        kb = k_ref[pl.ds(kstart, tk), :]
        vb = v_ref[pl.ds(kstart, tk), :]
        for c in range(2):
            s = lax.dot_general(qs[c][r0:], kb, (((1,), (1,)), ((), ())),
                                preferred_element_type=F32)
            if mask is not None:
                s = jnp.where(mask, s, MASKED)
            m_prev = m_ref[c, r0:, :]
            m_new = jnp.maximum(m_prev, jnp.max(s, axis=1, keepdims=True))
            alpha = jnp.exp2(m_prev - m_new)
            p = jnp.exp2(s - jnp.concatenate([m_new] * (tk // LANES), axis=1))
            l_ref[c, r0:, :] = alpha * l_ref[c, r0:, :] + jnp.sum(p, axis=1, keepdims=True)
            acc_ref[c, r0:, :] = alpha * acc_ref[c, r0:, :] + jnp.dot(
                p.astype(BF16), vb, preferred_element_type=F32)
            m_ref[c, r0:, :] = m_new

    def body(j, carry):
        block(0, pl.multiple_of(j * tk, tk), None)
        return carry

    lax.fori_loop(0, qi * (tq // tk), body, 0)
    for jj in range(tq // tk):
        r0 = jj * tk
        block(r0, pl.multiple_of(qi * tq + jj * tk, tk), _chunk_mask(tq - r0, tk, r0, jj * tk))
    _attn_finish(acc_ref[0], l_ref[0], acc_ref[1], l_ref[1], lam_ref[0], g_ref, o_ref)


def _attn_unshifted_kernel(lam_ref, q_ref, k_ref, v_ref, g_ref, o_ref, l_ref, acc_ref,
                           *, tq, tk, half):
    qi = pl.program_id(2)
    qs = _split_maps(q_ref[...], half)
    l_ref[...] = jnp.zeros(l_ref.shape, F32)
    acc_ref[...] = jnp.zeros(acc_ref.shape, F32)

    def block(r0, rows, kstart, ksize, mask):
        kb = k_ref[pl.ds(kstart, ksize), :]
        vb = v_ref[pl.ds(kstart, ksize), :]
        for c in range(2):
            s = lax.dot_general(qs[c][r0:r0 + rows], kb, (((1,), (1,)), ((), ())),
                                preferred_element_type=F32)
            p = jnp.exp2(s)
            if mask is not None:
                p = jnp.where(mask, p, 0.0)
            part = p[:, 0:LANES]
            for k in range(1, ksize // LANES):
                part = part + p[:, k * LANES:(k + 1) * LANES]
            l_ref[c, r0:r0 + rows, :] += part
            acc_ref[c, r0:r0 + rows, :] += jnp.dot(p.astype(BF16), vb,
                                                   preferred_element_type=F32)

    def body(j, carry):
        block(0, tq, pl.multiple_of(j * tk, tk), tk, None)
        return carry

    lax.fori_loop(0, qi * (tq // tk), body, 0)
    for jj in range(tq // tk):
        r0 = jj * tk
        block(r0, tq - r0, pl.multiple_of(qi * tq + jj * tk, tk), tk,
              _chunk_mask(tq - r0, tk, r0, jj * tk))
    l0 = jnp.sum(l_ref[0], axis=1, keepdims=True)
    l1 = jnp.sum(l_ref[1], axis=1, keepdims=True)
    _attn_finish(acc_ref[0], l0, acc_ref[1], l1, lam_ref[0], g_ref, o_ref)


def _attention(lam, proj, sub_g, *, unshifted, batch, seq, heads, vdim, qcol, kcol, vcol):
    tq, tk = (TQ_UNSHIFTED, TK_UNSHIFTED) if unshifted else (TQ_ATTN, TK_ATTN)
    nq = seq // tq
    acc = pltpu.VMEM((2, tq, vdim), F32)
    stat = pltpu.VMEM((2, tq, LANES), F32)
    if unshifted:
        kern = functools.partial(_attn_unshifted_kernel, tq=tq, tk=tk, half=vdim // 2)
        scratch = [stat, acc]
        name = "diffattn_unshifted"
    else:
        kern = functools.partial(_attn_kernel, tq=tq, tk=tk, half=vdim // 2)
        scratch = [stat, stat, acc]
        name = "diffattn"
    return pl.pallas_call(
        kern,
        out_shape=jax.ShapeDtypeStruct((batch * seq, heads * vdim), BF16),
        grid=(batch, heads, nq),
        in_specs=[
            pl.BlockSpec(memory_space=pltpu.SMEM),
            pl.BlockSpec((tq, vdim), lambda b, h, i: (b * nq + i, qcol + h)),
            pl.BlockSpec((seq, vdim), lambda b, h, i: (b, kcol + h)),
            pl.BlockSpec((seq, vdim), lambda b, h, i: (b, vcol + h)),
            pl.BlockSpec((1, vdim), lambda b, h, i: (0, 0)),
        ],
        out_specs=pl.BlockSpec((tq, vdim), lambda b, h, i: (b * nq + i, h)),
        scratch_shapes=scratch,
        compiler_params=_params("parallel", "parallel", "arbitrary"),
        name=name,
    )(lam, proj, proj, proj, sub_g)


def _outproj_kernel(p_ref, y_ref, a_ref, h_ref, d_ref, wglu_ref, bglu_ref, vgain_ref, ws_ref,
                    bs_ref, wo_ref, g2_ref, ho_ref, xn_ref, sg_ref,
                    *, w_ssm, w_gmlp, win, heads):
    p = p_ref[...]
    u = p[:, 0:w_ssm].astype(F32)
    ug = p[:, w_ssm:w_ssm + w_gmlp].astype(F32)
    vg = p[:, w_ssm + w_gmlp:].astype(F32)

    y = jax.nn.gelu(y_ref[...].astype(F32) + d_ref[...] * u)
    z = jnp.dot(y.astype(BF16), wglu_ref[...], preferred_element_type=F32) + bglu_ref[...]
    y_ssm = (y * jax.nn.sigmoid(z)).astype(BF16)

    v = _rms(vg, vgain_ref[...])
    head_shift = (w_gmlp // heads).bit_length() - 1
    head_of_lane = lax.broadcasted_iota(jnp.int32, (win, w_gmlp), 1) >> head_shift
    for w in range(p.shape[0] // win):
        vw = v[w * win:(w + 1) * win, :]
        mixed = bs_ref[...]
        for hh in range(heads):
            vh = jnp.where(head_of_lane == hh, vw, 0.0).astype(BF16)
            mixed = mixed + jnp.dot(ws_ref[hh], vh, preferred_element_type=F32)
        sg_ref[w * win:(w + 1) * win, :] = (ug[w * win:(w + 1) * win, :] * mixed).astype(BF16)

    out = jnp.dot(y_ssm, wo_ref[0:w_ssm, :], preferred_element_type=F32)
    out = out + jnp.dot(sg_ref[...], wo_ref[w_ssm:w_ssm + w_gmlp, :], preferred_element_type=F32)
    out = out + jnp.dot(a_ref[...], wo_ref[w_ssm + w_gmlp:, :], preferred_element_type=F32)
    h_new = h_ref[...] + out
    ho_ref[...] = h_new
    xn_ref[...] = _rms(h_new, g2_ref[...]).astype(BF16)


def _outproj(proj, y_tb, attn, h, d_skip, w_glu, b_glu, v_gain, w_s, b_s, w_out, g2,
             *, seq, w_ssm, w_gmlp):
    n, d = h.shape
    tm = TM_PROJ
    tps = seq // tm
    heads, win, _ = w_s.shape
    w_diff = attn.shape[1]
    front = w_ssm + 2 * w_gmlp
    assert (w_gmlp // heads) & (w_gmlp // heads - 1) == 0
    kern = functools.partial(_outproj_kernel, w_ssm=w_ssm, w_gmlp=w_gmlp, win=win, heads=heads)
    const2 = lambda i: (0, 0)
    return pl.pallas_call(
        kern,
        out_shape=(jax.ShapeDtypeStruct((n, d), F32), jax.ShapeDtypeStruct((n, d), BF16)),
        grid=(n // tm,),
        in_specs=[
            pl.BlockSpec((tm, front), lambda i: (i, 0)),
            pl.BlockSpec((tm, w_ssm), lambda i: (i % tps, i // tps)),
            pl.BlockSpec((tm, w_diff), lambda i: (i, 0)),
            pl.BlockSpec((tm, d), lambda i: (i, 0)),
            pl.BlockSpec((1, w_ssm), const2),
            pl.BlockSpec((w_ssm, w_ssm), const2),
            pl.BlockSpec((1, w_ssm), const2),
            pl.BlockSpec((1, w_gmlp), const2),
            pl.BlockSpec((heads, win, win), lambda i: (0, 0, 0)),
            pl.BlockSpec((win, w_gmlp), const2),
            pl.BlockSpec((w_ssm + w_gmlp + w_diff, d), const2),
            pl.BlockSpec((1, d), const2),
        ],
        out_specs=(pl.BlockSpec((tm, d), lambda i: (i, 0)), pl.BlockSpec((tm, d), lambda i: (i, 0))),
        scratch_shapes=[pltpu.VMEM((tm, w_gmlp), BF16)],
        compiler_params=_params("parallel"),
        name="outproj",
    )(proj, y_tb, attn, h, d_skip, w_glu, b_glu, v_gain, w_s, b_s, w_out, g2)


def _ffn_kernel(x_ref, halo_ref, h_ref, wg_ref, wv_ref, cwg_ref, cwv_ref, cbg_ref, cbv_ref,
                wd_ref, o_ref, sg_ref, sv_ref, *, tm, tiles_per_seq, taps):
    i = pl.program_id(0)

    @pl.when(pl.program_id(1) == 0)
    def _():
        o_ref[...] = h_ref[...]

    halo = halo_ref[...]
    halo = jnp.where(i % tiles_per_seq == 0, jnp.zeros_like(halo), halo)
    x = x_ref[...]

    def conv(w_ref, s_ref, cw_ref, cb_ref):
        s_ref[0:SUBLANES, :] = jnp.dot(halo, w_ref[...], preferred_element_type=F32)
        s_ref[SUBLANES:, :] = jnp.dot(x, w_ref[...], preferred_element_type=F32)
        acc = cb_ref[...]
        for k in range(taps):
            acc = acc + cw_ref[k:k + 1, :] * s_ref[pl.ds(SUBLANES - (taps - 1) + k, tm), :]
        return acc

    gate = conv(wg_ref, sg_ref, cwg_ref, cbg_ref)
    half_val = conv(wv_ref, sv_ref, cwv_ref, cbv_ref)
    inner = gate * (GELU_C + (GELU_C * GELU_K) * (gate * gate))
    a = (gate * (1.0 + jnp.tanh(inner)) * half_val).astype(BF16)
    o_ref[...] += jnp.dot(a, wd_ref[...], preferred_element_type=F32)


def _ffn(xn, h, w_up, conv_w, conv_b, w_down, *, seq):
    n, d = h.shape
    f = w_down.shape[0]
    taps = conv_w.shape[0]
    tm, tf = TM_FFN, TF_FFN
    nf = f // tf
    tiles_per_seq = seq // tm
    kern = functools.partial(_ffn_kernel, tm=tm, tiles_per_seq=tiles_per_seq, taps=taps)
    halo_blocks = tm // SUBLANES
    half = jnp.concatenate([jnp.ones((f,), F32), jnp.full((f,), 0.5, F32)])
    conv_w = conv_w * half
    conv_b = conv_b * half
    return pl.pallas_call(
        kern,
        out_shape=jax.ShapeDtypeStruct((n, d), F32),
        grid=(n // tm, nf),
        in_specs=[
            pl.BlockSpec((tm, d), lambda i, j: (i, 0)),
            pl.BlockSpec((SUBLANES, d), lambda i, j: (jnp.maximum(i * halo_blocks - 1, 0), 0)),
            pl.BlockSpec((tm, d), lambda i, j: (i, 0)),
            pl.BlockSpec((d, tf), lambda i, j: (0, j)),
            pl.BlockSpec((d, tf), lambda i, j: (0, nf + j)),
            pl.BlockSpec((taps, tf), lambda i, j: (0, j)),
            pl.BlockSpec((taps, tf), lambda i, j: (0, nf + j)),
            pl.BlockSpec((1, tf), lambda i, j: (0, j)),
            pl.BlockSpec((1, tf), lambda i, j: (0, nf + j)),
            pl.BlockSpec((tf, d), lambda i, j: (j, 0)),
        ],
        out_specs=pl.BlockSpec((tm, d), lambda i, j: (i, 0)),
        scratch_shapes=[pltpu.VMEM((tm + SUBLANES, tf), F32), pltpu.VMEM((tm + SUBLANES, tf), F32)],
        compiler_params=_params("parallel", "arbitrary"),
        name="convffn",
    )(xn, xn, h, w_up, w_up, conv_w, conv_w, conv_b, conv_b, w_down)


def _rope_tables(seq, head_dim, reps):
    half = head_dim // 2
    inv = ROPE_THETA ** (-jnp.arange(half, dtype=F32) / half)
    ang = jnp.arange(seq, dtype=F32)[:, None] * inv[None, :]
    ang = jnp.concatenate([ang, ang], axis=-1)
    sign = jnp.concatenate([-jnp.ones((half,), F32), jnp.ones((half,), F32)])
    return jnp.tile(jnp.cos(ang), (1, reps)), jnp.tile(jnp.sin(ang) * sign, (1, reps))


def kernel(x, attn_norm_g, w_in, ssm_a_re, ssm_a_im, ssm_log_dt, ssm_b_re, ssm_b_im, ssm_c_re, ssm_c_im, ssm_d, ssm_w_glu, ssm_b_glu, gmlp_v_g, gmlp_w_s, gmlp_b_s, q_norm_g, k_norm_g, lambda_q1, lambda_k1, lambda_q2, lambda_k2, subln_g, w_out, ffn_norm_g, w_up, conv_w, conv_b, w_down):
    batch, seq, d = x.shape
    depth, _, in_cols = w_in.shape
    w_ssm = ssm_d.shape[1]
    w_gmlp = gmlp_v_g.shape[1]
    w_diff = (in_cols - w_ssm - 2 * w_gmlp) // 3
    qk_dim = q_norm_g.shape[1]
    vdim = subln_g.shape[1]
    heads = w_diff // vdim
    gheads, win = gmlp_w_s.shape[1], gmlp_w_s.shape[2]
    o3 = w_ssm + 2 * w_gmlp
    o4 = o3 + w_diff
    o5 = o4 + w_diff
    assert vdim == LANES and 2 * qk_dim == vdim and batch == SUBLANES

    slab = 2 * LANES
    cos_t, sin_t = _rope_tables(seq, qk_dim, slab // qk_dim)
    lane = jnp.arange(slab)
    ones = (lane[:, None] // qk_dim == lane[None, :] // qk_dim).astype(BF16)
    q_scale = qk_dim ** -0.5 * math.log2(math.e)
    bt, ct, a_bar = _s5_operators(ssm_a_re, ssm_a_im, ssm_log_dt, ssm_b_re, ssm_b_im,
                                  ssm_c_re, ssm_c_im)
    pos_chunk = jnp.arange(win) // CHUNK
    sgu_mask = pos_chunk[None, :] <= pos_chunk[:, None]
    attn_args = dict(batch=batch, seq=seq, heads=heads, vdim=vdim,
                     qcol=o3 // vdim, kcol=o4 // vdim, vcol=o5 // vdim)

    h = x.reshape(batch * seq, d)
    for layer in range(depth):
        lambda_init = 0.8 - 0.6 * math.exp(-0.3 * layer)
        lam = (jnp.exp(jnp.sum(lambda_q1[layer] * lambda_k1[layer]))
               - jnp.exp(jnp.sum(lambda_q2[layer] * lambda_k2[layer])) + lambda_init)
        q_gain = q_norm_g[layer] * q_scale
        proj, u_tb = _inproj(
            h, attn_norm_g[layer][None], w_in[layer].astype(BF16), cos_t, sin_t,
            jnp.tile(q_gain, slab // qk_dim)[None],
            jnp.tile(k_norm_g[layer], slab // qk_dim)[None], ones,
            batch=batch, seq=seq, o1=w_ssm, o3=o3, o4=o4, o5=o5, head_dim=qk_dim)

        y_tb = _s5(u_tb.reshape(seq * batch, w_ssm), bt[layer], ct[layer], a_bar[layer],
                   bsz=batch).reshape(seq, batch * w_ssm)

        score_bound = qk_dim * jnp.max(jnp.abs(q_gain)) * jnp.max(jnp.abs(k_norm_g[layer]))
        attn = lax.cond(
            score_bound <= MAX_UNSHIFTED_SCORE,
            functools.partial(_attention, unshifted=True, **attn_args),
            functools.partial(_attention, unshifted=False, **attn_args),
            lam.reshape(1).astype(F32), proj, (subln_g[layer] * (1.0 - lambda_init))[None])

        w_s = jnp.where(sgu_mask[None], gmlp_w_s[layer], 0.0).astype(BF16)
        b_s = jnp.repeat(gmlp_b_s[layer].T, w_gmlp // gheads, axis=1)
        h, xn = _outproj(
            proj, y_tb, attn, h, ssm_d[layer][None], ssm_w_glu[layer].astype(BF16),
            ssm_b_glu[layer][None], gmlp_v_g[layer][None], w_s, b_s,
            w_out[layer].astype(BF16), ffn_norm_g[layer][None], seq=seq, w_ssm=w_ssm,
            w_gmlp=w_gmlp)
        h = _ffn(xn, h, w_up[layer].astype(BF16), conv_w[layer], conv_b[layer][None],
                 w_down[layer].astype(BF16), seq=seq)
    return h.reshape(batch, seq, d)
```

```python
import functools
import math

import jax
import jax.numpy as jnp
from jax import lax
from jax.experimental import pallas as pl
from jax.experimental.pallas import tpu as pltpu

BF16 = jnp.bfloat16
F32 = jnp.float32

EPS = 1e-6
ROPE_THETA = 10000.0
CHUNK = 64
LANES = 128
SUBLANES = 8
MASKED = -1e30
VMEM_LIMIT = 56 * 1024 * 1024
MAX_UNSHIFTED_SCORE = 60.0

TS_SSM = 128
TM_PROJ = 1024
SUB_PROJ = 512
TM_FFN = 1024
TF_FFN = 1408
GELU_C = math.sqrt(2.0 / math.pi)
GELU_K = 0.044715
TQ_ATTN = 512
TK_ATTN = 256
TQ_UNSHIFTED = 1024
TK_UNSHIFTED = 512


def _rms(x, g):
    return x * lax.rsqrt(jnp.mean(x * x, axis=-1, keepdims=True) + EPS) * g


def _params(*sem, flags=None):
    return pltpu.CompilerParams(dimension_semantics=sem, vmem_limit_bytes=VMEM_LIMIT, flags=flags)


def _inproj_kernel(h_ref, g_ref, w_ref, cos_ref, sin_ref, qg_ref, kg_ref, ones_ref,
                   o_ref, u_ref, *, o1, o3, o4, o5, slab, head_dim, sub):
    lane = lax.broadcasted_iota(jnp.int32, (sub, slab), 1)
    first_half = (lane & (head_dim - 1)) < head_dim // 2
    for r in range(0, h_ref.shape[0], sub):
        rows = slice(r, r + sub)
        xn = _rms(h_ref[rows, :], g_ref[...]).astype(BF16)
        front = jnp.dot(xn, w_ref[:, 0:o3], preferred_element_type=F32).astype(BF16)
        o_ref[rows, 0:o3] = front
        u_ref[rows, :] = front[:, 0:o1]
        o_ref[rows, o5:] = jnp.dot(xn, w_ref[:, o5:], preferred_element_type=F32).astype(BF16)
        cos = cos_ref[rows, :]
        sin = sin_ref[rows, :]
        for start, stop, gain_ref in ((o3, o4, qg_ref), (o4, o5, kg_ref)):
            for c in range(start, stop, slab):
                y = jnp.dot(xn, w_ref[:, c:c + slab], preferred_element_type=F32)
                ss = jnp.dot((y * y).astype(BF16), ones_ref[...], preferred_element_type=F32)
                y = y * lax.rsqrt(ss * (1.0 / head_dim) + EPS) * gain_ref[...]
                rot = jnp.where(first_half,
                                pltpu.roll(y, slab - head_dim // 2, 1),
                                pltpu.roll(y, head_dim // 2, 1))
                o_ref[rows, c:c + slab] = (y * cos + rot * sin).astype(BF16)


def _inproj(h, g, w, cos_t, sin_t, qg, kg, ones, *, batch, seq, o1, o3, o4, o5, head_dim):
    n, d = h.shape
    cols = w.shape[1]
    tm = TM_PROJ
    slab = cos_t.shape[1]
    tps = seq // tm
    kern = functools.partial(_inproj_kernel, o1=o1, o3=o3, o4=o4, o5=o5, slab=slab,
                             head_dim=head_dim, sub=SUB_PROJ)
    return pl.pallas_call(
        kern,
        out_shape=(jax.ShapeDtypeStruct((n, cols), BF16),
                   jax.ShapeDtypeStruct((seq, batch * o1), BF16)),
        grid=(n // tm,),
        in_specs=[
            pl.BlockSpec((tm, d), lambda i: (i, 0)),
            pl.BlockSpec((1, d), lambda i: (0, 0)),
            pl.BlockSpec((d, cols), lambda i: (0, 0)),
            pl.BlockSpec((tm, slab), lambda i: (i % tps, 0)),
            pl.BlockSpec((tm, slab), lambda i: (i % tps, 0)),
            pl.BlockSpec((1, slab), lambda i: (0, 0)),
            pl.BlockSpec((1, slab), lambda i: (0, 0)),
            pl.BlockSpec((slab, slab), lambda i: (0, 0)),
        ],
        out_specs=(pl.BlockSpec((tm, cols), lambda i: (i, 0)),
                   pl.BlockSpec((tm, o1), lambda i: (i % tps, i // tps))),
        compiler_params=_params("parallel"),
        name="inproj",
    )(h, g, w, cos_t, sin_t, qg, kg, ones)


def _s5_kernel(u_ref, bt_ref, ct_ref, a_ref, y_ref, z_ref, x_ref, *, ts, bsz, ns):
    @pl.when(pl.program_id(0) == 0)
    def _():
        x_ref[...] = jnp.zeros(x_ref.shape, F32)

    half = ts * bsz // 2
    for r in (0, half):
        z_ref[r:r + half, :] = jnp.dot(u_ref[r:r + half, :], bt_ref[...],
                                       preferred_element_type=F32)
    ar = jnp.broadcast_to(a_ref[0:1, :], (bsz, ns))
    ai = jnp.broadcast_to(a_ref[1:2, :], (bsz, ns))

    def body(t, carry):
        xr, xi = carry
        r = pl.multiple_of(t * bsz, bsz)
        nr = ar * xr - ai * xi + z_ref[pl.ds(r, bsz), 0:ns]
        ni = ar * xi + ai * xr + z_ref[pl.ds(r, bsz), ns:]
        z_ref[pl.ds(r, bsz), 0:ns] = nr
        z_ref[pl.ds(r, bsz), ns:] = ni
        return nr, ni

    xr, xi = lax.fori_loop(0, ts, body, (x_ref[:, 0:ns], x_ref[:, ns:]), unroll=4)
    x_ref[:, 0:ns] = xr
    x_ref[:, ns:] = xi
    for r in (0, half):
        y_ref[r:r + half, :] = jnp.dot(z_ref[r:r + half, :].astype(BF16), ct_ref[...],
                                       preferred_element_type=F32).astype(BF16)


def _s5(u_tb, bt, ct, a, *, bsz):
    rows, width = u_tb.shape
    ns = a.shape[1]
    ts = TS_SSM
    kern = functools.partial(_s5_kernel, ts=ts, bsz=bsz, ns=ns)
    return pl.pallas_call(
        kern,
        out_shape=jax.ShapeDtypeStruct((rows, width), BF16),
        grid=(rows // (ts * bsz),),
        in_specs=[
            pl.BlockSpec((ts * bsz, width), lambda i: (i, 0)),
            pl.BlockSpec((width, 2 * ns), lambda i: (0, 0)),
            pl.BlockSpec((2 * ns, width), lambda i: (0, 0)),
            pl.BlockSpec((2, ns), lambda i: (0, 0)),
        ],
        out_specs=pl.BlockSpec((ts * bsz, width), lambda i: (i, 0)),
        scratch_shapes=[pltpu.VMEM((ts * bsz, 2 * ns), F32), pltpu.VMEM((bsz, 2 * ns), F32)],
        compiler_params=_params("arbitrary"),
        name="s5",
    )(u_tb, bt, ct, a)


def _s5_operators(a_re, a_im, log_dt, b_re, b_im, c_re, c_im):
    nl, ng, p = a_re.shape
    c = b_re.shape[-1]
    dt = jnp.exp(log_dt)[..., None]
    mag = jnp.exp(a_re * dt)
    ar = mag * jnp.cos(a_im * dt)
    ai = mag * jnp.sin(a_im * dt)
    den = a_re * a_re + a_im * a_im
    kr = ((ar - 1.0) * a_re + ai * a_im) / den
    ki = (ai * a_re - (ar - 1.0) * a_im) / den
    bbr = kr[..., None] * b_re - ki[..., None] * b_im
    bbi = kr[..., None] * b_im + ki[..., None] * b_re
    eye = jnp.eye(ng, dtype=F32)

    def expand_in(b):
        return jnp.einsum("lgpc,gh->lgchp", b, eye).reshape(nl, ng * c, ng * p)

    def expand_out(m):
        return jnp.einsum("lgcp,gh->lgphc", m, eye).reshape(nl, ng * p, ng * c)

    bt = jnp.concatenate([expand_in(bbr), expand_in(bbi)], axis=-1)
    ct = jnp.concatenate([expand_out(c_re), -expand_out(c_im)], axis=1)
    a = jnp.stack([ar.reshape(nl, ng * p), ai.reshape(nl, ng * p)], axis=1)
    return bt.astype(BF16), ct.astype(BF16), a


def _split_maps(q, half):
    lane = lax.broadcasted_iota(jnp.int32, q.shape, 1)
    zero = jnp.zeros_like(q)
    return jnp.where(lane < half, q, zero), jnp.where(lane >= half, q, zero)


def _chunk_mask(rows, cols, r0, c0):
    shift = CHUNK.bit_length() - 1
    row = lax.broadcasted_iota(jnp.int32, (rows, cols), 0) + r0
    col = lax.broadcasted_iota(jnp.int32, (rows, cols), 1) + c0
    return (col >> shift) <= (row >> shift)


def _attn_finish(acc0, l0, acc1, l1, lam, g_ref, o_ref):
    o = acc0 / l0 - lam * (acc1 / l1)
    o_ref[...] = _rms(o, g_ref[...]).astype(BF16)


def _attn_kernel(lam_ref, q_ref, k_ref, v_ref, g_ref, o_ref, m_ref, l_ref, acc_ref,
                 *, tq, tk, half):
    qi = pl.program_id(2)
    qs = _split_maps(q_ref[...], half)
    m_ref[...] = jnp.full(m_ref.shape, MASKED, F32)
    l_ref[...] = jnp.zeros(l_ref.shape, F32)
    acc_ref[...] = jnp.zeros(acc_ref.shape, F32)

    def block(r0, kstart, mask):
        kb = k_ref[pl.ds(kstart, tk), :]
        vb = v_ref[pl.ds(kstart, tk), :]
        for c in range(2):
            s = lax.dot_general(qs[c][r0:], kb, (((1,), (1,)), ((), ())),
                                preferred_element_type=F32)
            if mask is not None:
                s = jnp.where(mask, s, MASKED)
            m_prev = m_ref[c, r0:, :]
            m_new = jnp.maximum(m_prev, jnp.max(s, axis=1, keepdims=True))
            alpha = jnp.exp2(m_prev - m_new)
            p = jnp.exp2(s - jnp.concatenate([m_new] * (tk // LANES), axis=1))
            l_ref[c, r0:, :] = alpha * l_ref[c, r0:, :] + jnp.sum(p, axis=1, keepdims=True)
            acc_ref[c, r0:, :] = alpha * acc_ref[c, r0:, :] + jnp.dot(
                p.astype(BF16), vb, preferred_element_type=F32)
            m_ref[c, r0:, :] = m_new

    def body(j, carry):
        block(0, pl.multiple_of(j * tk, tk), None)
        return carry

    lax.fori_loop(0, qi * (tq // tk), body, 0)
    for jj in range(tq // tk):
        r0 = jj * tk
        block(r0, pl.multiple_of(qi * tq + jj * tk, tk), _chunk_mask(tq - r0, tk, r0, jj * tk))
    _attn_finish(acc_ref[0], l_ref[0], acc_ref[1], l_ref[1], lam_ref[0], g_ref, o_ref)


def _attn_unshifted_kernel(lam_ref, q_ref, k_ref, v_ref, g_ref, o_ref, l_ref, acc_ref,
                           *, tq, tk, half):
    qi = pl.program_id(2)
    qs = _split_maps(q_ref[...], half)
    l_ref[...] = jnp.zeros(l_ref.shape, F32)
    acc_ref[...] = jnp.zeros(acc_ref.shape, F32)

    def block(r0, rows, kstart, ksize, mask):
        kb = k_ref[pl.ds(kstart, ksize), :]
        vb = v_ref[pl.ds(kstart, ksize), :]
        for c in range(2):
            s = lax.dot_general(qs[c][r0:r0 + rows], kb, (((1,), (1,)), ((), ())),
                                preferred_element_type=F32)
            p = jnp.exp2(s)
            if mask is not None:
                p = jnp.where(mask, p, 0.0)
            part = p[:, 0:LANES]
            for k in range(1, ksize // LANES):
                part = part + p[:, k * LANES:(k + 1) * LANES]
            l_ref[c, r0:r0 + rows, :] += part
            acc_ref[c, r0:r0 + rows, :] += jnp.dot(p.astype(BF16), vb,
                                                   preferred_element_type=F32)

    def body(j, carry):
        block(0, tq, pl.multiple_of(j * tk, tk), tk, None)
        return carry

    lax.fori_loop(0, qi * (tq // tk), body, 0)
    for jj in range(tq // tk):
        r0 = jj * tk
        block(r0, tq - r0, pl.multiple_of(qi * tq + jj * tk, tk), tk,
              _chunk_mask(tq - r0, tk, r0, jj * tk))
    l0 = jnp.sum(l_ref[0], axis=1, keepdims=True)
    l1 = jnp.sum(l_ref[1], axis=1, keepdims=True)
    _attn_finish(acc_ref[0], l0, acc_ref[1], l1, lam_ref[0], g_ref, o_ref)


def _attention(lam, proj, sub_g, *, unshifted, batch, seq, heads, vdim, qcol, kcol, vcol):
    tq, tk = (TQ_UNSHIFTED, TK_UNSHIFTED) if unshifted else (TQ_ATTN, TK_ATTN)
    nq = seq // tq
    acc = pltpu.VMEM((2, tq, vdim), F32)
    stat = pltpu.VMEM((2, tq, LANES), F32)
    if unshifted:
        kern = functools.partial(_attn_unshifted_kernel, tq=tq, tk=tk, half=vdim // 2)
        scratch = [stat, acc]
        name = "diffattn_unshifted"
    else:
        kern = functools.partial(_attn_kernel, tq=tq, tk=tk, half=vdim // 2)
        scratch = [stat, stat, acc]
        name = "diffattn"
    return pl.pallas_call(
        kern,
        out_shape=jax.ShapeDtypeStruct((batch * seq, heads * vdim), BF16),
        grid=(batch, heads, nq),
        in_specs=[
            pl.BlockSpec(memory_space=pltpu.SMEM),
            pl.BlockSpec((tq, vdim), lambda b, h, i: (b * nq + i, qcol + h)),
            pl.BlockSpec((seq, vdim), lambda b, h, i: (b, kcol + h)),
            pl.BlockSpec((seq, vdim), lambda b, h, i: (b, vcol + h)),
            pl.BlockSpec((1, vdim), lambda b, h, i: (0, 0)),
        ],
        out_specs=pl.BlockSpec((tq, vdim), lambda b, h, i: (b * nq + i, h)),
        scratch_shapes=scratch,
        compiler_params=_params("parallel", "parallel", "arbitrary"),
        name=name,
    )(lam, proj, proj, proj, sub_g)


def _outproj_kernel(p_ref, y_ref, a_ref, h_ref, d_ref, wglu_ref, bglu_ref, vgain_ref, ws_ref,
                    bs_ref, wo_ref, g2_ref, ho_ref, xn_ref, sg_ref,
                    *, w_ssm, w_gmlp, win, heads, sub):
    head_shift = (w_gmlp // heads).bit_length() - 1
    head_of_lane = lax.broadcasted_iota(jnp.int32, (win, w_gmlp), 1) >> head_shift
    for r in range(0, h_ref.shape[0], sub):
        rows = slice(r, r + sub)
        p = p_ref[rows, :]
        u = p[:, 0:w_ssm].astype(F32)
        ug = p[:, w_ssm:w_ssm + w_gmlp].astype(F32)
        vg = p[:, w_ssm + w_gmlp:].astype(F32)

        y = jax.nn.gelu(y_ref[rows, :].astype(F32) + d_ref[...] * u)
        z = jnp.dot(y.astype(BF16), wglu_ref[...], preferred_element_type=F32) + bglu_ref[...]
        y_ssm = (y * jax.nn.sigmoid(z)).astype(BF16)

        v = _rms(vg, vgain_ref[...])
        for w in range(sub // win):
            vw = v[w * win:(w + 1) * win, :]
            mixed = bs_ref[...]
            for hh in range(heads):
                vh = jnp.where(head_of_lane == hh, vw, 0.0).astype(BF16)
                mixed = mixed + jnp.dot(ws_ref[hh], vh, preferred_element_type=F32)
            sg_ref[r + w * win:r + (w + 1) * win, :] = (
                ug[w * win:(w + 1) * win, :] * mixed).astype(BF16)

        out = jnp.dot(y_ssm, wo_ref[0:w_ssm, :], preferred_element_type=F32)
        out = out + jnp.dot(sg_ref[rows, :], wo_ref[w_ssm:w_ssm + w_gmlp, :],
                            preferred_element_type=F32)
        out = out + jnp.dot(a_ref[rows, :], wo_ref[w_ssm + w_gmlp:, :],
                            preferred_element_type=F32)
        h_new = h_ref[rows, :] + out
        ho_ref[rows, :] = h_new
        xn_ref[rows, :] = _rms(h_new, g2_ref[...]).astype(BF16)


def _outproj(proj, y_tb, attn, h, d_skip, w_glu, b_glu, v_gain, w_s, b_s, w_out, g2,
             *, seq, w_ssm, w_gmlp):
    n, d = h.shape
    tm = TM_PROJ
    tps = seq // tm
    heads, win, _ = w_s.shape
    w_diff = attn.shape[1]
    front = w_ssm + 2 * w_gmlp
    assert (w_gmlp // heads) & (w_gmlp // heads - 1) == 0
    kern = functools.partial(_outproj_kernel, w_ssm=w_ssm, w_gmlp=w_gmlp, win=win, heads=heads,
                             sub=SUB_PROJ)
    const2 = lambda i: (0, 0)
    return pl.pallas_call(
        kern,
        out_shape=(jax.ShapeDtypeStruct((n, d), F32), jax.ShapeDtypeStruct((n, d), BF16)),
        grid=(n // tm,),
        in_specs=[
            pl.BlockSpec((tm, front), lambda i: (i, 0)),
            pl.BlockSpec((tm, w_ssm), lambda i: (i % tps, i // tps)),
            pl.BlockSpec((tm, w_diff), lambda i: (i, 0)),
            pl.BlockSpec((tm, d), lambda i: (i, 0)),
            pl.BlockSpec((1, w_ssm), const2),
            pl.BlockSpec((w_ssm, w_ssm), const2),
            pl.BlockSpec((1, w_ssm), const2),
            pl.BlockSpec((1, w_gmlp), const2),
            pl.BlockSpec((heads, win, win), lambda i: (0, 0, 0)),
            pl.BlockSpec((win, w_gmlp), const2),
            pl.BlockSpec((w_ssm + w_gmlp + w_diff, d), const2),
            pl.BlockSpec((1, d), const2),
        ],
        out_specs=(pl.BlockSpec((tm, d), lambda i: (i, 0)), pl.BlockSpec((tm, d), lambda i: (i, 0))),
        scratch_shapes=[pltpu.VMEM((tm, w_gmlp), BF16)],
        compiler_params=_params("parallel"),
        name="outproj",
    )(proj, y_tb, attn, h, d_skip, w_glu, b_glu, v_gain, w_s, b_s, w_out, g2)


def _ffn_kernel(x_ref, halo_ref, h_ref, wg_ref, wv_ref, cwg_ref, cwv_ref, cbg_ref, cbv_ref,
                wd_ref, o_ref, sg_ref, sv_ref, *, tm, tiles_per_seq, taps):
    i = pl.program_id(0)

    @pl.when(pl.program_id(1) == 0)
    def _():
        o_ref[...] = h_ref[...]

    halo = halo_ref[...]
    halo = jnp.where(i % tiles_per_seq == 0, jnp.zeros_like(halo), halo)
    x = x_ref[...]

    def conv(w_ref, s_ref, cw_ref, cb_ref):
        s_ref[0:SUBLANES, :] = jnp.dot(halo, w_ref[...], preferred_element_type=F32)
        s_ref[SUBLANES:, :] = jnp.dot(x, w_ref[...], preferred_element_type=F32)
        acc = cb_ref[...]
        for k in range(taps):
            acc = acc + cw_ref[k:k + 1, :] * s_ref[pl.ds(SUBLANES - (taps - 1) + k, tm), :]
        return acc

    gate = conv(wg_ref, sg_ref, cwg_ref, cbg_ref)
    half_val = conv(wv_ref, sv_ref, cwv_ref, cbv_ref)
    inner = gate * (GELU_C + (GELU_C * GELU_K) * (gate * gate))
    a = (gate * (1.0 + jnp.tanh(inner)) * half_val).astype(BF16)
    o_ref[...] += jnp.dot(a, wd_ref[...], preferred_element_type=F32)


def _ffn(xn, h, w_up, conv_w, conv_b, w_down, *, seq):
    n, d = h.shape
    f = w_down.shape[0]
    taps = conv_w.shape[0]
    tm, tf = TM_FFN, TF_FFN
    nf = f // tf
    tiles_per_seq = seq // tm
    kern = functools.partial(_ffn_kernel, tm=tm, tiles_per_seq=tiles_per_seq, taps=taps)
    halo_blocks = tm // SUBLANES
    weight_mode = pl.Buffered(1) if nf == 1 else None
    half = jnp.concatenate([jnp.ones((f,), F32), jnp.full((f,), 0.5, F32)])
    conv_w = conv_w * half
    conv_b = conv_b * half
    return pl.pallas_call(
        kern,
        out_shape=jax.ShapeDtypeStruct((n, d), F32),
        grid=(n // tm, nf),
        in_specs=[
            pl.BlockSpec((tm, d), lambda i, j: (i, 0)),
            pl.BlockSpec((SUBLANES, d), lambda i, j: (jnp.maximum(i * halo_blocks - 1, 0), 0)),
            pl.BlockSpec((tm, d), lambda i, j: (i, 0)),
            pl.BlockSpec((d, tf), lambda i, j: (0, j), pipeline_mode=weight_mode),
            pl.BlockSpec((d, tf), lambda i, j: (0, nf + j), pipeline_mode=weight_mode),
            pl.BlockSpec((taps, tf), lambda i, j: (0, j)),
            pl.BlockSpec((taps, tf), lambda i, j: (0, nf + j)),
            pl.BlockSpec((1, tf), lambda i, j: (0, j)),
            pl.BlockSpec((1, tf), lambda i, j: (0, nf + j)),
            pl.BlockSpec((tf, d), lambda i, j: (j, 0), pipeline_mode=weight_mode),
        ],
        out_specs=pl.BlockSpec((tm, d), lambda i, j: (i, 0)),
        scratch_shapes=[pltpu.VMEM((tm + SUBLANES, tf), F32), pltpu.VMEM((tm + SUBLANES, tf), F32)],
        compiler_params=_params("parallel", "arbitrary"),
        name="convffn",
    )(xn, xn, h, w_up, w_up, conv_w, conv_w, conv_b, conv_b, w_down)


def _rope_tables(seq, head_dim, reps):
    half = head_dim // 2
    inv = ROPE_THETA ** (-jnp.arange(half, dtype=F32) / half)
    ang = jnp.arange(seq, dtype=F32)[:, None] * inv[None, :]
    ang = jnp.concatenate([ang, ang], axis=-1)
    sign = jnp.concatenate([-jnp.ones((half,), F32), jnp.ones((half,), F32)])
    return jnp.tile(jnp.cos(ang), (1, reps)), jnp.tile(jnp.sin(ang) * sign, (1, reps))


def kernel(x, attn_norm_g, w_in, ssm_a_re, ssm_a_im, ssm_log_dt, ssm_b_re, ssm_b_im, ssm_c_re, ssm_c_im, ssm_d, ssm_w_glu, ssm_b_glu, gmlp_v_g, gmlp_w_s, gmlp_b_s, q_norm_g, k_norm_g, lambda_q1, lambda_k1, lambda_q2, lambda_k2, subln_g, w_out, ffn_norm_g, w_up, conv_w, conv_b, w_down):
    batch, seq, d = x.shape
    depth, _, in_cols = w_in.shape
    w_ssm = ssm_d.shape[1]
    w_gmlp = gmlp_v_g.shape[1]
    w_diff = (in_cols - w_ssm - 2 * w_gmlp) // 3
    qk_dim = q_norm_g.shape[1]
    vdim = subln_g.shape[1]
    heads = w_diff // vdim
    gheads, win = gmlp_w_s.shape[1], gmlp_w_s.shape[2]
    o3 = w_ssm + 2 * w_gmlp
    o4 = o3 + w_diff
    o5 = o4 + w_diff
    assert vdim == LANES and 2 * qk_dim == vdim and batch == SUBLANES

    slab = w_diff
    cos_t, sin_t = _rope_tables(seq, qk_dim, slab // qk_dim)
    lane = jnp.arange(slab)
    ones = (lane[:, None] // qk_dim == lane[None, :] // qk_dim).astype(BF16)
    q_scale = qk_dim ** -0.5 * math.log2(math.e)
    bt, ct, a_bar = _s5_operators(ssm_a_re, ssm_a_im, ssm_log_dt, ssm_b_re, ssm_b_im,
                                  ssm_c_re, ssm_c_im)
    pos_chunk = jnp.arange(win) // CHUNK
    sgu_mask = pos_chunk[None, :] <= pos_chunk[:, None]
    attn_args = dict(batch=batch, seq=seq, heads=heads, vdim=vdim,
                     qcol=o3 // vdim, kcol=o4 // vdim, vcol=o5 // vdim)

    h = x.reshape(batch * seq, d)
    for layer in range(depth):
        lambda_init = 0.8 - 0.6 * math.exp(-0.3 * layer)
        lam = (jnp.exp(jnp.sum(lambda_q1[layer] * lambda_k1[layer]))
               - jnp.exp(jnp.sum(lambda_q2[layer] * lambda_k2[layer])) + lambda_init)
        q_gain = q_norm_g[layer] * q_scale
        proj, u_tb = _inproj(
            h, attn_norm_g[layer][None], w_in[layer].astype(BF16), cos_t, sin_t,
            jnp.tile(q_gain, slab // qk_dim)[None],
            jnp.tile(k_norm_g[layer], slab // qk_dim)[None], ones,
            batch=batch, seq=seq, o1=w_ssm, o3=o3, o4=o4, o5=o5, head_dim=qk_dim)

        y_tb = _s5(u_tb.reshape(seq * batch, w_ssm), bt[layer], ct[layer], a_bar[layer],
                   bsz=batch).reshape(seq, batch * w_ssm)

        score_bound = qk_dim * jnp.max(jnp.abs(q_gain)) * jnp.max(jnp.abs(k_norm_g[layer]))
        attn = lax.cond(
            score_bound <= MAX_UNSHIFTED_SCORE,
            functools.partial(_attention, unshifted=True, **attn_args),
            functools.partial(_attention, unshifted=False, **attn_args),
            lam.reshape(1).astype(F32), proj, (subln_g[layer] * (1.0 - lambda_init))[None])

        w_s = jnp.where(sgu_mask[None], gmlp_w_s[layer], 0.0).astype(BF16)
        b_s = jnp.repeat(gmlp_b_s[layer].T, w_gmlp // gheads, axis=1)
        h, xn = _outproj(
            proj, y_tb, attn, h, ssm_d[layer][None], ssm_w_glu[layer].astype(BF16),
            ssm_b_glu[layer][None], gmlp_v_g[layer][None], w_s, b_s,
            w_out[layer].astype(BF16), ffn_norm_g[layer][None], seq=seq, w_ssm=w_ssm,
            w_gmlp=w_gmlp)
        h = _ffn(xn, h, w_up[layer].astype(BF16), conv_w[layer], conv_b[layer][None],
                 w_down[layer].astype(BF16), seq=seq)
    return h.reshape(batch, seq, d)
```

```python
import functools
import math

import jax
import jax.numpy as jnp
from jax import lax
from jax.experimental import pallas as pl
from jax.experimental.pallas import tpu as pltpu

BF16 = jnp.bfloat16
F32 = jnp.float32

EPS = 1e-6
ROPE_THETA = 10000.0
CHUNK = 64
LANES = 128
SUBLANES = 8
MASKED = -1e30
VMEM_LIMIT = 56 * 1024 * 1024
MAX_UNSHIFTED_SCORE = 60.0

TS_SSM = 128
TM_PROJ = 1024
SUB_PROJ = 512
TM_FFN = 1024
TF_FFN = 1408
GELU_C = math.sqrt(2.0 / math.pi)
GELU_K = 0.044715
TQ_ATTN = 512
TK_ATTN = 256
TQ_UNSHIFTED = 1024
TK_UNSHIFTED = 512


def _rms(x, g):
    return x * lax.rsqrt(jnp.mean(x * x, axis=-1, keepdims=True) + EPS) * g


def _params(*sem, flags=None):
    return pltpu.CompilerParams(dimension_semantics=sem, vmem_limit_bytes=VMEM_LIMIT, flags=flags)


def _inproj_kernel(h_ref, g_ref, w_ref, cos_ref, sin_ref, qg_ref, kg_ref, ones_ref,
                   o_ref, u_ref, *, o1, o3, o4, o5, slab, head_dim, sub):
    lane = lax.broadcasted_iota(jnp.int32, (sub, slab), 1)
    first_half = (lane & (head_dim - 1)) < head_dim // 2
    for r in range(0, h_ref.shape[0], sub):
        rows = slice(r, r + sub)
        xn = _rms(h_ref[rows, :], g_ref[...]).astype(BF16)
        front = jnp.dot(xn, w_ref[:, 0:o3], preferred_element_type=F32).astype(BF16)
        o_ref[rows, 0:o3] = front
        u_ref[rows, :] = front[:, 0:o1]
        o_ref[rows, o5:] = jnp.dot(xn, w_ref[:, o5:], preferred_element_type=F32).astype(BF16)
        cos = cos_ref[rows, :]
        sin = sin_ref[rows, :]
        for start, stop, gain_ref in ((o3, o4, qg_ref), (o4, o5, kg_ref)):
            for c in range(start, stop, slab):
                y = jnp.dot(xn, w_ref[:, c:c + slab], preferred_element_type=F32)
                ss = jnp.dot((y * y).astype(BF16), ones_ref[...], preferred_element_type=F32)
                y = y * lax.rsqrt(ss * (1.0 / head_dim) + EPS) * gain_ref[...]
                rot = jnp.where(first_half,
                                pltpu.roll(y, slab - head_dim // 2, 1),
                                pltpu.roll(y, head_dim // 2, 1))
                o_ref[rows, c:c + slab] = (y * cos + rot * sin).astype(BF16)


def _inproj(h, g, w, cos_t, sin_t, qg, kg, ones, *, batch, seq, o1, o3, o4, o5, head_dim):
    n, d = h.shape
    cols = w.shape[1]
    tm = TM_PROJ
    slab = cos_t.shape[1]
    tps = seq // tm
    kern = functools.partial(_inproj_kernel, o1=o1, o3=o3, o4=o4, o5=o5, slab=slab,
                             head_dim=head_dim, sub=SUB_PROJ)
    return pl.pallas_call(
        kern,
        out_shape=(jax.ShapeDtypeStruct((n, cols), BF16),
                   jax.ShapeDtypeStruct((seq, batch * o1), BF16)),
        grid=(n // tm,),
        in_specs=[
            pl.BlockSpec((tm, d), lambda i: (i, 0)),
            pl.BlockSpec((1, d), lambda i: (0, 0)),
            pl.BlockSpec((d, cols), lambda i: (0, 0)),
            pl.BlockSpec((tm, slab), lambda i: (i % tps, 0)),
            pl.BlockSpec((tm, slab), lambda i: (i % tps, 0)),
            pl.BlockSpec((1, slab), lambda i: (0, 0)),
            pl.BlockSpec((1, slab), lambda i: (0, 0)),
            pl.BlockSpec((slab, slab), lambda i: (0, 0)),
        ],
        out_specs=(pl.BlockSpec((tm, cols), lambda i: (i, 0)),
                   pl.BlockSpec((tm, o1), lambda i: (i % tps, i // tps))),
        compiler_params=_params("parallel"),
        name="inproj",
    )(h, g, w, cos_t, sin_t, qg, kg, ones)


def _s5_kernel(u_ref, bt_ref, ct_ref, a_ref, y_ref, z_ref, x_ref, *, ts, bsz, ns):
    @pl.when(pl.program_id(0) == 0)
    def _():
        x_ref[...] = jnp.zeros(x_ref.shape, F32)

    half = ts * bsz // 2
    for r in (0, half):
        z_ref[r:r + half, :] = jnp.dot(u_ref[r:r + half, :], bt_ref[...],
                                       preferred_element_type=F32)
    ar = jnp.broadcast_to(a_ref[0:1, :], (bsz, ns))
    ai = jnp.broadcast_to(a_ref[1:2, :], (bsz, ns))

    def body(t, carry):
        xr, xi = carry
        r = pl.multiple_of(t * bsz, bsz)
        nr = ar * xr - ai * xi + z_ref[pl.ds(r, bsz), 0:ns]
        ni = ar * xi + ai * xr + z_ref[pl.ds(r, bsz), ns:]
        z_ref[pl.ds(r, bsz), 0:ns] = nr
        z_ref[pl.ds(r, bsz), ns:] = ni
        return nr, ni

    xr, xi = lax.fori_loop(0, ts, body, (x_ref[:, 0:ns], x_ref[:, ns:]), unroll=4)
    x_ref[:, 0:ns] = xr
    x_ref[:, ns:] = xi
    for r in (0, half):
        y_ref[r:r + half, :] = jnp.dot(z_ref[r:r + half, :].astype(BF16), ct_ref[...],
                                       preferred_element_type=F32).astype(BF16)


def _s5(u_tb, bt, ct, a, *, bsz):
    rows, width = u_tb.shape
    ns = a.shape[1]
    ts = TS_SSM
    kern = functools.partial(_s5_kernel, ts=ts, bsz=bsz, ns=ns)
    return pl.pallas_call(
        kern,
        out_shape=jax.ShapeDtypeStruct((rows, width), BF16),
        grid=(rows // (ts * bsz),),
        in_specs=[
            pl.BlockSpec((ts * bsz, width), lambda i: (i, 0)),
            pl.BlockSpec((width, 2 * ns), lambda i: (0, 0)),
            pl.BlockSpec((2 * ns, width), lambda i: (0, 0)),
            pl.BlockSpec((2, ns), lambda i: (0, 0)),
        ],
        out_specs=pl.BlockSpec((ts * bsz, width), lambda i: (i, 0)),
        scratch_shapes=[pltpu.VMEM((ts * bsz, 2 * ns), F32), pltpu.VMEM((bsz, 2 * ns), F32)],
        compiler_params=_params("arbitrary"),
        name="s5",
    )(u_tb, bt, ct, a)


def _s5_operators(a_re, a_im, log_dt, b_re, b_im, c_re, c_im):
    nl, ng, p = a_re.shape
    c = b_re.shape[-1]
    dt = jnp.exp(log_dt)[..., None]
    mag = jnp.exp(a_re * dt)
    ar = mag * jnp.cos(a_im * dt)
    ai = mag * jnp.sin(a_im * dt)
    den = a_re * a_re + a_im * a_im
    kr = ((ar - 1.0) * a_re + ai * a_im) / den
    ki = (ai * a_re - (ar - 1.0) * a_im) / den
    bbr = kr[..., None] * b_re - ki[..., None] * b_im
    bbi = kr[..., None] * b_im + ki[..., None] * b_re
    eye = jnp.eye(ng, dtype=F32)

    def expand_in(b):
        return jnp.einsum("lgpc,gh->lgchp", b, eye).reshape(nl, ng * c, ng * p)

    def expand_out(m):
        return jnp.einsum("lgcp,gh->lgphc", m, eye).reshape(nl, ng * p, ng * c)

    bt = jnp.concatenate([expand_in(bbr), expand_in(bbi)], axis=-1)
    ct = jnp.concatenate([expand_out(c_re), -expand_out(c_im)], axis=1)
    a = jnp.stack([ar.reshape(nl, ng * p), ai.reshape(nl, ng * p)], axis=1)
    return bt.astype(BF16), ct.astype(BF16), a


def _split_maps(q, half):
    lane = lax.broadcasted_iota(jnp.int32, q.shape, 1)
    zero = jnp.zeros_like(q)
    return jnp.where(lane < half, q, zero), jnp.where(lane >= half, q, zero)


def _chunk_mask(rows, cols, r0, c0):
    shift = CHUNK.bit_length() - 1
    row = lax.broadcasted_iota(jnp.int32, (rows, cols), 0) + r0
    col = lax.broadcasted_iota(jnp.int32, (rows, cols), 1) + c0
    return (col >> shift) <= (row >> shift)


def _attn_finish(acc0, l0, acc1, l1, lam, g_ref, o_ref):
    o = acc0 / l0 - lam * (acc1 / l1)
    o_ref[...] = _rms(o, g_ref[...]).astype(BF16)


def _attn_kernel(lam_ref, q_ref, k_ref, v_ref, g_ref, o_ref, m_ref, l_ref, acc_ref,
                 *, tq, tk, half):
    qi = pl.program_id(2)
    qs = _split_maps(q_ref[...], half)
    m_ref[...] = jnp.full(m_ref.shape, MASKED, F32)
    l_ref[...] = jnp.zeros(l_ref.shape, F32)
    acc_ref[...] = jnp.zeros(acc_ref.shape, F32)

    def block(r0, kstart, mask):
        kb = k_ref[pl.ds(kstart, tk), :]
        vb = v_ref[pl.ds(kstart, tk), :]
        for c in range(2):
            s = lax.dot_general(qs[c][r0:], kb, (((1,), (1,)), ((), ())),
                                preferred_element_type=F32)
            if mask is not None:
                s = jnp.where(mask, s, MASKED)
            m_prev = m_ref[c, r0:, :]
            m_new = jnp.maximum(m_prev, jnp.max(s, axis=1, keepdims=True))
            alpha = jnp.exp2(m_prev - m_new)
            p = jnp.exp2(s - jnp.concatenate([m_new] * (tk // LANES), axis=1))
            l_ref[c, r0:, :] = alpha * l_ref[c, r0:, :] + jnp.sum(p, axis=1, keepdims=True)
            acc_ref[c, r0:, :] = alpha * acc_ref[c, r0:, :] + jnp.dot(
                p.astype(BF16), vb, preferred_element_type=F32)
            m_ref[c, r0:, :] = m_new

    def body(j, carry):
        block(0, pl.multiple_of(j * tk, tk), None)
        return carry

    lax.fori_loop(0, qi * (tq // tk), body, 0)
    for jj in range(tq // tk):
        r0 = jj * tk
        block(r0, pl.multiple_of(qi * tq + jj * tk, tk), _chunk_mask(tq - r0, tk, r0, jj * tk))
    _attn_finish(acc_ref[0], l_ref[0], acc_ref[1], l_ref[1], lam_ref[0], g_ref, o_ref)


def _attn_unshifted_kernel(lam_ref, q_ref, k_ref, v_ref, g_ref, o_ref, l_ref, acc_ref,
                           *, tq, tk, half):
    qi = pl.program_id(2)
    qs = _split_maps(q_ref[...], half)

    def clear():
        l_ref[...] = jnp.zeros(l_ref.shape, F32)
        acc_ref[...] = jnp.zeros(acc_ref.shape, F32)

    pl.when((pl.program_id(0) == 0) & (pl.program_id(1) == 0) & (qi == 0))(clear)

    def block(r0, rows, kstart, ksize, mask):
        kb = k_ref[pl.ds(kstart, ksize), :]
        vb = v_ref[pl.ds(kstart, ksize), :]
        for c in range(2):
            s = lax.dot_general(qs[c][r0:r0 + rows], kb, (((1,), (1,)), ((), ())),
                                preferred_element_type=F32)
            p = jnp.exp2(s)
            if mask is not None:
                p = jnp.where(mask, p, 0.0)
            part = p[:, 0:LANES]
            for k in range(1, ksize // LANES):
                part = part + p[:, k * LANES:(k + 1) * LANES]
            l_ref[c, r0:r0 + rows, :] += part
            acc_ref[c, r0:r0 + rows, :] += jnp.dot(p.astype(BF16), vb,
                                                   preferred_element_type=F32)

    def body(j, carry):
        block(0, tq, pl.multiple_of(j * tq, tq), tq, None)
        return carry

    lax.fori_loop(0, qi, body, 0)
    for jj in range(tq // tk):
        r0 = jj * tk
        block(r0, tq - r0, pl.multiple_of(qi * tq + jj * tk, tk), tk,
              _chunk_mask(tq - r0, tk, r0, jj * tk))
    l0 = jnp.sum(l_ref[0], axis=1, keepdims=True)
    l1 = jnp.sum(l_ref[1], axis=1, keepdims=True)
    _attn_finish(acc_ref[0], l0, acc_ref[1], l1, lam_ref[0], g_ref, o_ref)
    clear()


def _attention(lam, proj, sub_g, *, unshifted, batch, seq, heads, vdim, qcol, kcol, vcol):
    tq, tk = (TQ_UNSHIFTED, TK_UNSHIFTED) if unshifted else (TQ_ATTN, TK_ATTN)
    nq = seq // tq
    acc = pltpu.VMEM((2, tq, vdim), F32)
    stat = pltpu.VMEM((2, tq, LANES), F32)
    if unshifted:
        kern = functools.partial(_attn_unshifted_kernel, tq=tq, tk=tk, half=vdim // 2)
        scratch = [stat, acc]
        name = "diffattn_unshifted"
    else:
        kern = functools.partial(_attn_kernel, tq=tq, tk=tk, half=vdim // 2)
        scratch = [stat, stat, acc]
        name = "diffattn"
    return pl.pallas_call(
        kern,
        out_shape=jax.ShapeDtypeStruct((batch * seq, heads * vdim), BF16),
        grid=(batch, heads, nq),
        in_specs=[
            pl.BlockSpec(memory_space=pltpu.SMEM),
            pl.BlockSpec((tq, vdim), lambda b, h, i: (b * nq + i, qcol + h)),
            pl.BlockSpec((seq, vdim), lambda b, h, i: (b, kcol + h)),
            pl.BlockSpec((seq, vdim), lambda b, h, i: (b, vcol + h)),
            pl.BlockSpec((1, vdim), lambda b, h, i: (0, 0)),
        ],
        out_specs=pl.BlockSpec((tq, vdim), lambda b, h, i: (b * nq + i, h)),
        scratch_shapes=scratch,
        compiler_params=_params("arbitrary", "arbitrary", "arbitrary"),
        name=name,
    )(lam, proj, proj, proj, sub_g)


def _outproj_kernel(p_ref, y_ref, a_ref, h_ref, d_ref, wglu_ref, bglu_ref, vgain_ref, ws_ref,
                    bs_ref, wo_ref, g2_ref, ho_ref, xn_ref, sg_ref,
                    *, w_ssm, w_gmlp, win, heads, sub):
    head_shift = (w_gmlp // heads).bit_length() - 1
    head_of_lane = lax.broadcasted_iota(jnp.int32, (win, w_gmlp), 1) >> head_shift
    for r in range(0, h_ref.shape[0], sub):
        rows = slice(r, r + sub)
        p = p_ref[rows, :]
        u = p[:, 0:w_ssm].astype(F32)
        ug = p[:, w_ssm:w_ssm + w_gmlp].astype(F32)
        vg = p[:, w_ssm + w_gmlp:].astype(F32)

        y = jax.nn.gelu(y_ref[rows, :].astype(F32) + d_ref[...] * u)
        z = jnp.dot(y.astype(BF16), wglu_ref[...], preferred_element_type=F32) + bglu_ref[...]
        y_ssm = (y * jax.nn.sigmoid(z)).astype(BF16)

        v = _rms(vg, vgain_ref[...])
        for w in range(sub // win):
            vw = v[w * win:(w + 1) * win, :]
            mixed = bs_ref[...]
            for hh in range(heads):
                vh = jnp.where(head_of_lane == hh, vw, 0.0).astype(BF16)
                mixed = mixed + jnp.dot(ws_ref[hh], vh, preferred_element_type=F32)
            sg_ref[r + w * win:r + (w + 1) * win, :] = (
                ug[w * win:(w + 1) * win, :] * mixed).astype(BF16)

        out = jnp.dot(y_ssm, wo_ref[0:w_ssm, :], preferred_element_type=F32)
        out = out + jnp.dot(sg_ref[rows, :], wo_ref[w_ssm:w_ssm + w_gmlp, :],
                            preferred_element_type=F32)
        out = out + jnp.dot(a_ref[rows, :], wo_ref[w_ssm + w_gmlp:, :],
                            preferred_element_type=F32)
        h_new = h_ref[rows, :] + out
        ho_ref[rows, :] = h_new
        xn_ref[rows, :] = _rms(h_new, g2_ref[...]).astype(BF16)


def _outproj(proj, y_tb, attn, h, d_skip, w_glu, b_glu, v_gain, w_s, b_s, w_out, g2,
             *, seq, w_ssm, w_gmlp):
    n, d = h.shape
    tm = TM_PROJ
    tps = seq // tm
    heads, win, _ = w_s.shape
    w_diff = attn.shape[1]
    front = w_ssm + 2 * w_gmlp
    assert (w_gmlp // heads) & (w_gmlp // heads - 1) == 0
    kern = functools.partial(_outproj_kernel, w_ssm=w_ssm, w_gmlp=w_gmlp, win=win, heads=heads,
                             sub=SUB_PROJ)
    const2 = lambda i: (0, 0)
    return pl.pallas_call(
        kern,
        out_shape=(jax.ShapeDtypeStruct((n, d), F32), jax.ShapeDtypeStruct((n, d), BF16)),
        grid=(n // tm,),
        in_specs=[
            pl.BlockSpec((tm, front), lambda i: (i, 0)),
            pl.BlockSpec((tm, w_ssm), lambda i: (i % tps, i // tps)),
            pl.BlockSpec((tm, w_diff), lambda i: (i, 0)),
            pl.BlockSpec((tm, d), lambda i: (i, 0)),
            pl.BlockSpec((1, w_ssm), const2),
            pl.BlockSpec((w_ssm, w_ssm), const2),
            pl.BlockSpec((1, w_ssm), const2),
            pl.BlockSpec((1, w_gmlp), const2),
            pl.BlockSpec((heads, win, win), lambda i: (0, 0, 0)),
            pl.BlockSpec((win, w_gmlp), const2),
            pl.BlockSpec((w_ssm + w_gmlp + w_diff, d), const2),
            pl.BlockSpec((1, d), const2),
        ],
        out_specs=(pl.BlockSpec((tm, d), lambda i: (i, 0)), pl.BlockSpec((tm, d), lambda i: (i, 0))),
        scratch_shapes=[pltpu.VMEM((tm, w_gmlp), BF16)],
        compiler_params=_params("parallel"),
        name="outproj",
    )(proj, y_tb, attn, h, d_skip, w_glu, b_glu, v_gain, w_s, b_s, w_out, g2)


def _ffn_kernel(x_ref, halo_ref, h_ref, wg_ref, wv_ref, cwg_ref, cwv_ref, cbg_ref, cbv_ref,
                wd_ref, o_ref, sg_ref, sv_ref, *, tm, tiles_per_seq, taps):
    i = pl.program_id(0)

    @pl.when(pl.program_id(1) == 0)
    def _():
        o_ref[...] = h_ref[...]

    halo = halo_ref[...]
    halo = jnp.where(i % tiles_per_seq == 0, jnp.zeros_like(halo), halo)
    x = x_ref[...]

    def conv(w_ref, s_ref, cw_ref, cb_ref):
        s_ref[0:SUBLANES, :] = jnp.dot(halo, w_ref[...], preferred_element_type=F32)
        s_ref[SUBLANES:, :] = jnp.dot(x, w_ref[...], preferred_element_type=F32)
        acc = cb_ref[...]
        for k in range(taps):
            acc = acc + cw_ref[k:k + 1, :] * s_ref[pl.ds(SUBLANES - (taps - 1) + k, tm), :]
        return acc

    gate = conv(wg_ref, sg_ref, cwg_ref, cbg_ref)
    half_val = conv(wv_ref, sv_ref, cwv_ref, cbv_ref)
    inner = gate * (GELU_C + (GELU_C * GELU_K) * (gate * gate))
    a = (gate * (1.0 + jnp.tanh(inner)) * half_val).astype(BF16)
    o_ref[...] += jnp.dot(a, wd_ref[...], preferred_element_type=F32)


def _ffn(xn, h, w_up, conv_w, conv_b, w_down, *, seq):
    n, d = h.shape
    f = w_down.shape[0]
    taps = conv_w.shape[0]
    tm, tf = TM_FFN, TF_FFN
    nf = f // tf
    tiles_per_seq = seq // tm
    kern = functools.partial(_ffn_kernel, tm=tm, tiles_per_seq=tiles_per_seq, taps=taps)
    halo_blocks = tm // SUBLANES
    weight_mode = pl.Buffered(1) if nf == 1 else None
    half = jnp.concatenate([jnp.ones((f,), F32), jnp.full((f,), 0.5, F32)])
    conv_w = conv_w * half
    conv_b = conv_b * half
    return pl.pallas_call(
        kern,
        out_shape=jax.ShapeDtypeStruct((n, d), F32),
        grid=(n // tm, nf),
        in_specs=[
            pl.BlockSpec((tm, d), lambda i, j: (i, 0)),
            pl.BlockSpec((SUBLANES, d), lambda i, j: (jnp.maximum(i * halo_blocks - 1, 0), 0)),
            pl.BlockSpec((tm, d), lambda i, j: (i, 0)),
            pl.BlockSpec((d, tf), lambda i, j: (0, j), pipeline_mode=weight_mode),
            pl.BlockSpec((d, tf), lambda i, j: (0, nf + j), pipeline_mode=weight_mode),
            pl.BlockSpec((taps, tf), lambda i, j: (0, j)),
            pl.BlockSpec((taps, tf), lambda i, j: (0, nf + j)),
            pl.BlockSpec((1, tf), lambda i, j: (0, j)),
            pl.BlockSpec((1, tf), lambda i, j: (0, nf + j)),
            pl.BlockSpec((tf, d), lambda i, j: (j, 0), pipeline_mode=weight_mode),
        ],
        out_specs=pl.BlockSpec((tm, d), lambda i, j: (i, 0)),
        scratch_shapes=[pltpu.VMEM((tm + SUBLANES, tf), F32), pltpu.VMEM((tm + SUBLANES, tf), F32)],
        compiler_params=_params("parallel", "arbitrary"),
        name="convffn",
    )(xn, xn, h, w_up, w_up, conv_w, conv_w, conv_b, conv_b, w_down)


def _rope_tables(seq, head_dim, reps):
    half = head_dim // 2
    inv = ROPE_THETA ** (-jnp.arange(half, dtype=F32) / half)
    ang = jnp.arange(seq, dtype=F32)[:, None] * inv[None, :]
    ang = jnp.concatenate([ang, ang], axis=-1)
    sign = jnp.concatenate([-jnp.ones((half,), F32), jnp.ones((half,), F32)])
    return jnp.tile(jnp.cos(ang), (1, reps)), jnp.tile(jnp.sin(ang) * sign, (1, reps))


def kernel(x, attn_norm_g, w_in, ssm_a_re, ssm_a_im, ssm_log_dt, ssm_b_re, ssm_b_im, ssm_c_re, ssm_c_im, ssm_d, ssm_w_glu, ssm_b_glu, gmlp_v_g, gmlp_w_s, gmlp_b_s, q_norm_g, k_norm_g, lambda_q1, lambda_k1, lambda_q2, lambda_k2, subln_g, w_out, ffn_norm_g, w_up, conv_w, conv_b, w_down):
    batch, seq, d = x.shape
    depth, _, in_cols = w_in.shape
    w_ssm = ssm_d.shape[1]
    w_gmlp = gmlp_v_g.shape[1]
    w_diff = (in_cols - w_ssm - 2 * w_gmlp) // 3
    qk_dim = q_norm_g.shape[1]
    vdim = subln_g.shape[1]
    heads = w_diff // vdim
    gheads, win = gmlp_w_s.shape[1], gmlp_w_s.shape[2]
    o3 = w_ssm + 2 * w_gmlp
    o4 = o3 + w_diff
    o5 = o4 + w_diff
    assert vdim == LANES and 2 * qk_dim == vdim and batch == SUBLANES

    slab = w_diff
    cos_t, sin_t = _rope_tables(seq, qk_dim, slab // qk_dim)
    lane = jnp.arange(slab)
    ones = (lane[:, None] // qk_dim == lane[None, :] // qk_dim).astype(BF16)
    q_scale = qk_dim ** -0.5 * math.log2(math.e)
    bt, ct, a_bar = _s5_operators(ssm_a_re, ssm_a_im, ssm_log_dt, ssm_b_re, ssm_b_im,
                                  ssm_c_re, ssm_c_im)
    pos_chunk = jnp.arange(win) // CHUNK
    sgu_mask = pos_chunk[None, :] <= pos_chunk[:, None]
    attn_args = dict(batch=batch, seq=seq, heads=heads, vdim=vdim,
                     qcol=o3 // vdim, kcol=o4 // vdim, vcol=o5 // vdim)

    h = x.reshape(batch * seq, d)
    for layer in range(depth):
        lambda_init = 0.8 - 0.6 * math.exp(-0.3 * layer)
        lam = (jnp.exp(jnp.sum(lambda_q1[layer] * lambda_k1[layer]))
               - jnp.exp(jnp.sum(lambda_q2[layer] * lambda_k2[layer])) + lambda_init)
        q_gain = q_norm_g[layer] * q_scale
        proj, u_tb = _inproj(
            h, attn_norm_g[layer][None], w_in[layer].astype(BF16), cos_t, sin_t,
            jnp.tile(q_gain, slab // qk_dim)[None],
            jnp.tile(k_norm_g[layer], slab // qk_dim)[None], ones,
            batch=batch, seq=seq, o1=w_ssm, o3=o3, o4=o4, o5=o5, head_dim=qk_dim)

        y_tb = _s5(u_tb.reshape(seq * batch, w_ssm), bt[layer], ct[layer], a_bar[layer],
                   bsz=batch).reshape(seq, batch * w_ssm)

        score_bound = qk_dim * jnp.max(jnp.abs(q_gain)) * jnp.max(jnp.abs(k_norm_g[layer]))
        attn = lax.cond(
            score_bound <= MAX_UNSHIFTED_SCORE,
            functools.partial(_attention, unshifted=True, **attn_args),
            functools.partial(_attention, unshifted=False, **attn_args),
            lam.reshape(1).astype(F32), proj, (subln_g[layer] * (1.0 - lambda_init))[None])

        w_s = jnp.where(sgu_mask[None], gmlp_w_s[layer], 0.0).astype(BF16)
        b_s = jnp.repeat(gmlp_b_s[layer].T, w_gmlp // gheads, axis=1)
        h, xn = _outproj(
            proj, y_tb, attn, h, ssm_d[layer][None], ssm_w_glu[layer].astype(BF16),
            ssm_b_glu[layer][None], gmlp_v_g[layer][None], w_s, b_s,
            w_out[layer].astype(BF16), ffn_norm_g[layer][None], seq=seq, w_ssm=w_ssm,
            w_gmlp=w_gmlp)
        h = _ffn(xn, h, w_up[layer].astype(BF16), conv_w[layer], conv_b[layer][None],
                 w_down[layer].astype(BF16), seq=seq)
    return h.reshape(batch, seq, d)
```

```python
import functools
import math

import jax
import jax.numpy as jnp
from jax import lax
from jax.experimental import pallas as pl
from jax.experimental.pallas import tpu as pltpu

BF16 = jnp.bfloat16
F32 = jnp.float32

EPS = 1e-6
ROPE_THETA = 10000.0
CHUNK = 64
LANES = 128
SUBLANES = 8
MASKED = -1e30
VMEM_LIMIT = 56 * 1024 * 1024
MAX_UNSHIFTED_SCORE = 60.0

TS_SSM = 256
TM_PROJ = 1024
SUB_PROJ = 512
TM_FFN = 1024
TF_FFN = 1408
GELU_C = math.sqrt(2.0 / math.pi)
GELU_K = 0.044715
TQ_ATTN = 512
TK_ATTN = 256
TQ_UNSHIFTED = 1024
TK_UNSHIFTED = 256


def _rms(x, g):
    return x * lax.rsqrt(jnp.mean(x * x, axis=-1, keepdims=True) + EPS) * g


def _params(*sem, flags=None):
    return pltpu.CompilerParams(dimension_semantics=sem, vmem_limit_bytes=VMEM_LIMIT, flags=flags)


def _inproj_kernel(h_ref, g_ref, w_ref, cos_ref, sin_ref, qg_ref, kg_ref, ones_ref,
                   o_ref, u_ref, *, o1, o3, o4, o5, slab, head_dim, sub):
    lane = lax.broadcasted_iota(jnp.int32, (sub, slab), 1)
    first_half = (lane & (head_dim - 1)) < head_dim // 2
    for r in range(0, h_ref.shape[0], sub):
        rows = slice(r, r + sub)
        xn = _rms(h_ref[rows, :], g_ref[...]).astype(BF16)
        front = jnp.dot(xn, w_ref[:, 0:o3], preferred_element_type=F32).astype(BF16)
        o_ref[rows, 0:o3] = front
        u_ref[rows, :] = front[:, 0:o1]
        o_ref[rows, o5:] = jnp.dot(xn, w_ref[:, o5:], preferred_element_type=F32).astype(BF16)
        cos = cos_ref[rows, :]
        sin = sin_ref[rows, :]
        for start, stop, gain_ref in ((o3, o4, qg_ref), (o4, o5, kg_ref)):
            for c in range(start, stop, slab):
                y = jnp.dot(xn, w_ref[:, c:c + slab], preferred_element_type=F32)
                y2 = (y * y).astype(BF16)
                piece = ones_ref.shape[0]
                ss = jnp.concatenate(
                    [jnp.dot(y2[:, q:q + piece], ones_ref[...], preferred_element_type=F32)
                     for q in range(0, slab, piece)], axis=1)
                y = y * lax.rsqrt(ss * (1.0 / head_dim) + EPS) * gain_ref[...]
                rot = jnp.where(first_half,
                                pltpu.roll(y, slab - head_dim // 2, 1),
                                pltpu.roll(y, head_dim // 2, 1))
                o_ref[rows, c:c + slab] = (y * cos + rot * sin).astype(BF16)


def _inproj(h, g, w, cos_t, sin_t, qg, kg, ones, *, batch, seq, o1, o3, o4, o5, head_dim):
    n, d = h.shape
    cols = w.shape[1]
    tm = TM_PROJ
    slab = cos_t.shape[1]
    tps = seq // tm
    kern = functools.partial(_inproj_kernel, o1=o1, o3=o3, o4=o4, o5=o5, slab=slab,
                             head_dim=head_dim, sub=SUB_PROJ)
    return pl.pallas_call(
        kern,
        out_shape=(jax.ShapeDtypeStruct((n, cols), BF16),
                   jax.ShapeDtypeStruct((seq, batch * o1), BF16)),
        grid=(n // tm,),
        in_specs=[
            pl.BlockSpec((tm, d), lambda i: (i, 0)),
            pl.BlockSpec((1, d), lambda i: (0, 0)),
            pl.BlockSpec((d, cols), lambda i: (0, 0)),
            pl.BlockSpec((tm, slab), lambda i: (i % tps, 0)),
            pl.BlockSpec((tm, slab), lambda i: (i % tps, 0)),
            pl.BlockSpec((1, slab), lambda i: (0, 0)),
            pl.BlockSpec((1, slab), lambda i: (0, 0)),
            pl.BlockSpec(ones.shape, lambda i: (0, 0)),
        ],
        out_specs=(pl.BlockSpec((tm, cols), lambda i: (i, 0)),
                   pl.BlockSpec((tm, o1), lambda i: (i % tps, i // tps))),
        compiler_params=_params("parallel"),
        name="inproj",
    )(h, g, w, cos_t, sin_t, qg, kg, ones)


def _s5_kernel(u_ref, bt_ref, ct_ref, a_ref, y_ref, z_ref, x_ref, *, ts, bsz, ns):
    @pl.when(pl.program_id(0) == 0)
    def _():
        x_ref[...] = jnp.zeros(x_ref.shape, F32)

    half = ts * bsz // 2
    for r in (0, half):
        z_ref[r:r + half, :] = jnp.dot(u_ref[r:r + half, :], bt_ref[...],
                                       preferred_element_type=F32)
    ar = jnp.broadcast_to(a_ref[0:1, :], (bsz, ns))
    ai = jnp.broadcast_to(a_ref[1:2, :], (bsz, ns))

    def body(t, carry):
        xr, xi = carry
        r = pl.multiple_of(t * bsz, bsz)
        nr = ar * xr - ai * xi + z_ref[pl.ds(r, bsz), 0:ns]
        ni = ar * xi + ai * xr + z_ref[pl.ds(r, bsz), ns:]
        z_ref[pl.ds(r, bsz), 0:ns] = nr
        z_ref[pl.ds(r, bsz), ns:] = ni
        return nr, ni

    xr, xi = lax.fori_loop(0, ts, body, (x_ref[:, 0:ns], x_ref[:, ns:]), unroll=4)
    x_ref[:, 0:ns] = xr
    x_ref[:, ns:] = xi
    for r in (0, half):
        y_ref[r:r + half, :] = jnp.dot(z_ref[r:r + half, :].astype(BF16), ct_ref[...],
                                       preferred_element_type=F32).astype(BF16)


def _s5(u_tb, bt, ct, a, *, bsz):
    rows, width = u_tb.shape
    ns = a.shape[1]
    ts = TS_SSM
    kern = functools.partial(_s5_kernel, ts=ts, bsz=bsz, ns=ns)
    return pl.pallas_call(
        kern,
        out_shape=jax.ShapeDtypeStruct((rows, width), BF16),
        grid=(rows // (ts * bsz),),
        in_specs=[
            pl.BlockSpec((ts * bsz, width), lambda i: (i, 0)),
            pl.BlockSpec((width, 2 * ns), lambda i: (0, 0)),
            pl.BlockSpec((2 * ns, width), lambda i: (0, 0)),
            pl.BlockSpec((2, ns), lambda i: (0, 0)),
        ],
        out_specs=pl.BlockSpec((ts * bsz, width), lambda i: (i, 0)),
        scratch_shapes=[pltpu.VMEM((ts * bsz, 2 * ns), F32), pltpu.VMEM((bsz, 2 * ns), F32)],
        compiler_params=_params("arbitrary"),
        name="s5",
    )(u_tb, bt, ct, a)


def _s5_operators(a_re, a_im, log_dt, b_re, b_im, c_re, c_im):
    nl, ng, p = a_re.shape
    c = b_re.shape[-1]
    dt = jnp.exp(log_dt)[..., None]
    mag = jnp.exp(a_re * dt)
    ar = mag * jnp.cos(a_im * dt)
    ai = mag * jnp.sin(a_im * dt)
    den = a_re * a_re + a_im * a_im
    kr = ((ar - 1.0) * a_re + ai * a_im) / den
    ki = (ai * a_re - (ar - 1.0) * a_im) / den
    bbr = kr[..., None] * b_re - ki[..., None] * b_im
    bbi = kr[..., None] * b_im + ki[..., None] * b_re
    eye = jnp.eye(ng, dtype=F32)

    def expand_in(b):
        return jnp.einsum("lgpc,gh->lgchp", b, eye).reshape(nl, ng * c, ng * p)

    def expand_out(m):
        return jnp.einsum("lgcp,gh->lgphc", m, eye).reshape(nl, ng * p, ng * c)

    bt = jnp.concatenate([expand_in(bbr), expand_in(bbi)], axis=-1)
    ct = jnp.concatenate([expand_out(c_re), -expand_out(c_im)], axis=1)
    a = jnp.stack([ar.reshape(nl, ng * p), ai.reshape(nl, ng * p)], axis=1)
    return bt.astype(BF16), ct.astype(BF16), a


def _split_maps(q, half):
    lane = lax.broadcasted_iota(jnp.int32, q.shape, 1)
    zero = jnp.zeros_like(q)
    return jnp.where(lane < half, q, zero), jnp.where(lane >= half, q, zero)


def _chunk_mask(rows, cols, r0, c0):
    shift = CHUNK.bit_length() - 1
    row = lax.broadcasted_iota(jnp.int32, (rows, cols), 0) + r0
    col = lax.broadcasted_iota(jnp.int32, (rows, cols), 1) + c0
    return (col >> shift) <= (row >> shift)


def _attn_finish(acc0, l0, acc1, l1, lam, g_ref, o_ref):
    o = acc0 / l0 - lam * (acc1 / l1)
    o_ref[...] = _rms(o, g_ref[...]).astype(BF16)


def _attn_kernel(lam_ref, q_ref, k_ref, v_ref, g_ref, o_ref, m_ref, l_ref, acc_ref,
                 *, tq, tk, half):
    qi = pl.program_id(2)
    qs = _split_maps(q_ref[...], half)
    m_ref[...] = jnp.full(m_ref.shape, MASKED, F32)
    l_ref[...] = jnp.zeros(l_ref.shape, F32)
    acc_ref[...] = jnp.zeros(acc_ref.shape, F32)

    def block(r0, kstart, mask):
        kb = k_ref[pl.ds(kstart, tk), :]
        vb = v_ref[pl.ds(kstart, tk), :]
        for c in range(2):
            s = lax.dot_general(qs[c][r0:], kb, (((1,), (1,)), ((), ())),
                                preferred_element_type=F32)
            if mask is not None:
                s = jnp.where(mask, s, MASKED)
            m_prev = m_ref[c, r0:, :]
            m_new = jnp.maximum(m_prev, jnp.max(s, axis=1, keepdims=True))
            alpha = jnp.exp2(m_prev - m_new)
            p = jnp.exp2(s - jnp.concatenate([m_new] * (tk // LANES), axis=1))
            l_ref[c, r0:, :] = alpha * l_ref[c, r0:, :] + jnp.sum(p, axis=1, keepdims=True)
            acc_ref[c, r0:, :] = alpha * acc_ref[c, r0:, :] + jnp.dot(
                p.astype(BF16), vb, preferred_element_type=F32)
            m_ref[c, r0:, :] = m_new

    def body(j, carry):
        block(0, pl.multiple_of(j * tk, tk), None)
        return carry

    lax.fori_loop(0, qi * (tq // tk), body, 0)
    for jj in range(tq // tk):
        r0 = jj * tk
        block(r0, pl.multiple_of(qi * tq + jj * tk, tk), _chunk_mask(tq - r0, tk, r0, jj * tk))
    _attn_finish(acc_ref[0], l_ref[0], acc_ref[1], l_ref[1], lam_ref[0], g_ref, o_ref)


def _attn_unshifted_kernel(lam_ref, q_ref, k_ref, v_ref, g_ref, o_ref, l_ref, acc_ref,
                           *, tq, tk, half):
    qi = pl.program_id(2)
    qs = _split_maps(q_ref[...], half)

    def clear():
        l_ref[...] = jnp.zeros(l_ref.shape, F32)
        acc_ref[...] = jnp.zeros(acc_ref.shape, F32)

    pl.when((pl.program_id(0) == 0) & (pl.program_id(1) == 0) & (qi == 0))(clear)

    def block(r0, rows, kstart, ksize, mask):
        kb = k_ref[pl.ds(kstart, ksize), :]
        vb = v_ref[pl.ds(kstart, ksize), :]
        for c in range(2):
            s = lax.dot_general(qs[c][r0:r0 + rows], kb, (((1,), (1,)), ((), ())),
                                preferred_element_type=F32)
            p = jnp.exp2(s)
            if mask is not None:
                p = jnp.where(mask, p, 0.0)
            part = p[:, 0:LANES]
            for k in range(1, ksize // LANES):
                part = part + p[:, k * LANES:(k + 1) * LANES]
            l_ref[c, r0:r0 + rows, :] += part
            acc_ref[c, r0:r0 + rows, :] += jnp.dot(p.astype(BF16), vb,
                                                   preferred_element_type=F32)

    def body(j, carry):
        block(0, tq, pl.multiple_of(j * tq, tq), tq, None)
        return carry

    lax.fori_loop(0, qi, body, 0)
    for jj in range(tq // tk):
        r0 = jj * tk
        block(r0, tq - r0, pl.multiple_of(qi * tq + jj * tk, tk), tk,
              _chunk_mask(tq - r0, tk, r0, jj * tk))
    l0 = jnp.sum(l_ref[0], axis=1, keepdims=True)
    l1 = jnp.sum(l_ref[1], axis=1, keepdims=True)
    _attn_finish(acc_ref[0], l0, acc_ref[1], l1, lam_ref[0], g_ref, o_ref)
    clear()


def _attention(lam, proj, sub_g, *, unshifted, batch, seq, heads, vdim, qcol, kcol, vcol):
    tq, tk = (TQ_UNSHIFTED, TK_UNSHIFTED) if unshifted else (TQ_ATTN, TK_ATTN)
    nq = seq // tq
    acc = pltpu.VMEM((2, tq, vdim), F32)
    stat = pltpu.VMEM((2, tq, LANES), F32)
    if unshifted:
        kern = functools.partial(_attn_unshifted_kernel, tq=tq, tk=tk, half=vdim // 2)
        scratch = [stat, acc]
        name = "diffattn_unshifted"
    else:
        kern = functools.partial(_attn_kernel, tq=tq, tk=tk, half=vdim // 2)
        scratch = [stat, stat, acc]
        name = "diffattn"
    return pl.pallas_call(
        kern,
        out_shape=jax.ShapeDtypeStruct((batch * seq, heads * vdim), BF16),
        grid=(batch, heads, nq),
        in_specs=[
            pl.BlockSpec(memory_space=pltpu.SMEM),
            pl.BlockSpec((tq, vdim), lambda b, h, i: (b * nq + i, qcol + h)),
            pl.BlockSpec((seq, vdim), lambda b, h, i: (b, kcol + h)),
            pl.BlockSpec((seq, vdim), lambda b, h, i: (b, vcol + h)),
            pl.BlockSpec((1, vdim), lambda b, h, i: (0, 0)),
        ],
        out_specs=pl.BlockSpec((tq, vdim), lambda b, h, i: (b * nq + i, h)),
        scratch_shapes=scratch,
        compiler_params=_params("arbitrary", "arbitrary", "arbitrary"),
        name=name,
    )(lam, proj, proj, proj, sub_g)


def _outproj_kernel(p_ref, y_ref, a_ref, h_ref, d_ref, wglu_ref, bglu_ref, vgain_ref, ws_ref,
                    bs_ref, wo_ref, g2_ref, ho_ref, xn_ref, sg_ref,
                    *, w_ssm, w_gmlp, win, heads, sub):
    head_shift = (w_gmlp // heads).bit_length() - 1
    head_of_lane = lax.broadcasted_iota(jnp.int32, (win, w_gmlp), 1) >> head_shift
    for r in range(0, h_ref.shape[0], sub):
        rows = slice(r, r + sub)
        p = p_ref[rows, :]
        u = p[:, 0:w_ssm].astype(F32)
        ug = p[:, w_ssm:w_ssm + w_gmlp].astype(F32)
        vg = p[:, w_ssm + w_gmlp:].astype(F32)

        y = jax.nn.gelu(y_ref[rows, :].astype(F32) + d_ref[...] * u)
        z = jnp.dot(y.astype(BF16), wglu_ref[...], preferred_element_type=F32) + bglu_ref[...]
        y_ssm = (y * jax.nn.sigmoid(z)).astype(BF16)

        v = _rms(vg, vgain_ref[...])
        for w in range(sub // win):
            vw = v[w * win:(w + 1) * win, :]
            mixed = bs_ref[...]
            for hh in range(heads):
                vh = jnp.where(head_of_lane == hh, vw, 0.0).astype(BF16)
                mixed = mixed + jnp.dot(ws_ref[hh], vh, preferred_element_type=F32)
            sg_ref[r + w * win:r + (w + 1) * win, :] = (
                ug[w * win:(w + 1) * win, :] * mixed).astype(BF16)

        out = jnp.dot(y_ssm, wo_ref[0:w_ssm, :], preferred_element_type=F32)
        out = out + jnp.dot(sg_ref[rows, :], wo_ref[w_ssm:w_ssm + w_gmlp, :],
                            preferred_element_type=F32)
        out = out + jnp.dot(a_ref[rows, :], wo_ref[w_ssm + w_gmlp:, :],
                            preferred_element_type=F32)
        h_new = h_ref[rows, :] + out
        ho_ref[rows, :] = h_new
        xn_ref[rows, :] = _rms(h_new, g2_ref[...]).astype(BF16)


def _outproj(proj, y_tb, attn, h, d_skip, w_glu, b_glu, v_gain, w_s, b_s, w_out, g2,
             *, seq, w_ssm, w_gmlp):
    n, d = h.shape
    tm = TM_PROJ
    tps = seq // tm
    heads, win, _ = w_s.shape
    w_diff = attn.shape[1]
    front = w_ssm + 2 * w_gmlp
    assert (w_gmlp // heads) & (w_gmlp // heads - 1) == 0
    kern = functools.partial(_outproj_kernel, w_ssm=w_ssm, w_gmlp=w_gmlp, win=win, heads=heads,
                             sub=SUB_PROJ)
    const2 = lambda i: (0, 0)
    return pl.pallas_call(
        kern,
        out_shape=(jax.ShapeDtypeStruct((n, d), F32), jax.ShapeDtypeStruct((n, d), BF16)),
        grid=(n // tm,),
        in_specs=[
            pl.BlockSpec((tm, front), lambda i: (i, 0)),
            pl.BlockSpec((tm, w_ssm), lambda i: (i % tps, i // tps)),
            pl.BlockSpec((tm, w_diff), lambda i: (i, 0)),
            pl.BlockSpec((tm, d), lambda i: (i, 0)),
            pl.BlockSpec((1, w_ssm), const2),
            pl.BlockSpec((w_ssm, w_ssm), const2),
            pl.BlockSpec((1, w_ssm), const2),
            pl.BlockSpec((1, w_gmlp), const2),
            pl.BlockSpec((heads, win, win), lambda i: (0, 0, 0)),
            pl.BlockSpec((win, w_gmlp), const2),
            pl.BlockSpec((w_ssm + w_gmlp + w_diff, d), const2),
            pl.BlockSpec((1, d), const2),
        ],
        out_specs=(pl.BlockSpec((tm, d), lambda i: (i, 0)), pl.BlockSpec((tm, d), lambda i: (i, 0))),
        scratch_shapes=[pltpu.VMEM((tm, w_gmlp), BF16)],
        compiler_params=_params("parallel"),
        name="outproj",
    )(proj, y_tb, attn, h, d_skip, w_glu, b_glu, v_gain, w_s, b_s, w_out, g2)


def _ffn_kernel(x_ref, halo_ref, h_ref, wg_ref, wv_ref, cwg_ref, cwv_ref, cbg_ref, cbv_ref,
                wd_ref, o_ref, sg_ref, sv_ref, *, tm, tiles_per_seq, taps):
    i = pl.program_id(0)

    @pl.when(pl.program_id(1) == 0)
    def _():
        o_ref[...] = h_ref[...]

    halo = halo_ref[...]
    halo = jnp.where(i % tiles_per_seq == 0, jnp.zeros_like(halo), halo)
    x = x_ref[...]

    def conv(w_ref, s_ref, cw_ref, cb_ref):
        s_ref[0:SUBLANES, :] = jnp.dot(halo, w_ref[...], preferred_element_type=F32)
        s_ref[SUBLANES:, :] = jnp.dot(x, w_ref[...], preferred_element_type=F32)
        acc = cb_ref[...]
        for k in range(taps):
            acc = acc + cw_ref[k:k + 1, :] * s_ref[pl.ds(SUBLANES - (taps - 1) + k, tm), :]
        return acc

    gate = conv(wg_ref, sg_ref, cwg_ref, cbg_ref)
    half_val = conv(wv_ref, sv_ref, cwv_ref, cbv_ref)
    inner = gate * (GELU_C + (GELU_C * GELU_K) * (gate * gate))
    a = (gate * (1.0 + jnp.tanh(inner)) * half_val).astype(BF16)
    o_ref[...] += jnp.dot(a, wd_ref[...], preferred_element_type=F32)


def _ffn(xn, h, w_up, conv_w, conv_b, w_down, *, seq):
    n, d = h.shape
    f = w_down.shape[0]
    taps = conv_w.shape[0]
    tm, tf = TM_FFN, TF_FFN
    nf = f // tf
    tiles_per_seq = seq // tm
    kern = functools.partial(_ffn_kernel, tm=tm, tiles_per_seq=tiles_per_seq, taps=taps)
    halo_blocks = tm // SUBLANES
    weight_mode = pl.Buffered(1) if nf == 1 else None
    half = jnp.concatenate([jnp.ones((f,), F32), jnp.full((f,), 0.5, F32)])
    conv_w = conv_w * half
    conv_b = conv_b * half
    return pl.pallas_call(
        kern,
        out_shape=jax.ShapeDtypeStruct((n, d), F32),
        grid=(n // tm, nf),
        in_specs=[
            pl.BlockSpec((tm, d), lambda i, j: (i, 0)),
            pl.BlockSpec((SUBLANES, d), lambda i, j: (jnp.maximum(i * halo_blocks - 1, 0), 0)),
            pl.BlockSpec((tm, d), lambda i, j: (i, 0)),
            pl.BlockSpec((d, tf), lambda i, j: (0, j), pipeline_mode=weight_mode),
            pl.BlockSpec((d, tf), lambda i, j: (0, nf + j), pipeline_mode=weight_mode),
            pl.BlockSpec((taps, tf), lambda i, j: (0, j)),
            pl.BlockSpec((taps, tf), lambda i, j: (0, nf + j)),
            pl.BlockSpec((1, tf), lambda i, j: (0, j)),
            pl.BlockSpec((1, tf), lambda i, j: (0, nf + j)),
            pl.BlockSpec((tf, d), lambda i, j: (j, 0), pipeline_mode=weight_mode),
        ],
        out_specs=pl.BlockSpec((tm, d), lambda i, j: (i, 0)),
        scratch_shapes=[pltpu.VMEM((tm + SUBLANES, tf), F32), pltpu.VMEM((tm + SUBLANES, tf), F32)],
        compiler_params=_params("parallel", "arbitrary"),
        name="convffn",
    )(xn, xn, h, w_up, w_up, conv_w, conv_w, conv_b, conv_b, w_down)


def _rope_tables(seq, head_dim, reps):
    half = head_dim // 2
    inv = ROPE_THETA ** (-jnp.arange(half, dtype=F32) / half)
    ang = jnp.arange(seq, dtype=F32)[:, None] * inv[None, :]
    ang = jnp.concatenate([ang, ang], axis=-1)
    sign = jnp.concatenate([-jnp.ones((half,), F32), jnp.ones((half,), F32)])
    return jnp.tile(jnp.cos(ang), (1, reps)), jnp.tile(jnp.sin(ang) * sign, (1, reps))


def kernel(x, attn_norm_g, w_in, ssm_a_re, ssm_a_im, ssm_log_dt, ssm_b_re, ssm_b_im, ssm_c_re, ssm_c_im, ssm_d, ssm_w_glu, ssm_b_glu, gmlp_v_g, gmlp_w_s, gmlp_b_s, q_norm_g, k_norm_g, lambda_q1, lambda_k1, lambda_q2, lambda_k2, subln_g, w_out, ffn_norm_g, w_up, conv_w, conv_b, w_down):
    batch, seq, d = x.shape
    depth, _, in_cols = w_in.shape
    w_ssm = ssm_d.shape[1]
    w_gmlp = gmlp_v_g.shape[1]
    w_diff = (in_cols - w_ssm - 2 * w_gmlp) // 3
    qk_dim = q_norm_g.shape[1]
    vdim = subln_g.shape[1]
    heads = w_diff // vdim
    gheads, win = gmlp_w_s.shape[1], gmlp_w_s.shape[2]
    o3 = w_ssm + 2 * w_gmlp
    o4 = o3 + w_diff
    o5 = o4 + w_diff
    assert vdim == LANES and 2 * qk_dim == vdim and batch == SUBLANES

    slab = w_diff
    cos_t, sin_t = _rope_tables(seq, qk_dim, slab // qk_dim)
    lane = jnp.arange(2 * LANES)
    ones = (lane[:, None] // qk_dim == lane[None, :] // qk_dim).astype(BF16)
    q_scale = qk_dim ** -0.5 * math.log2(math.e)
    bt, ct, a_bar = _s5_operators(ssm_a_re, ssm_a_im, ssm_log_dt, ssm_b_re, ssm_b_im,
                                  ssm_c_re, ssm_c_im)
    pos_chunk = jnp.arange(win) // CHUNK
    sgu_mask = pos_chunk[None, :] <= pos_chunk[:, None]
    attn_args = dict(batch=batch, seq=seq, heads=heads, vdim=vdim,
                     qcol=o3 // vdim, kcol=o4 // vdim, vcol=o5 // vdim)

    h = x.reshape(batch * seq, d)
    for layer in range(depth):
        lambda_init = 0.8 - 0.6 * math.exp(-0.3 * layer)
        lam = (jnp.exp(jnp.sum(lambda_q1[layer] * lambda_k1[layer]))
               - jnp.exp(jnp.sum(lambda_q2[layer] * lambda_k2[layer])) + lambda_init)
        q_gain = q_norm_g[layer] * q_scale
        proj, u_tb = _inproj(
            h, attn_norm_g[layer][None], w_in[layer].astype(BF16), cos_t, sin_t,
            jnp.tile(q_gain, slab // qk_dim)[None],
            jnp.tile(k_norm_g[layer], slab // qk_dim)[None], ones,
            batch=batch, seq=seq, o1=w_ssm, o3=o3, o4=o4, o5=o5, head_dim=qk_dim)

        y_tb = _s5(u_tb.reshape(seq * batch, w_ssm), bt[layer], ct[layer], a_bar[layer],
                   bsz=batch).reshape(seq, batch * w_ssm)

        score_bound = qk_dim * jnp.max(jnp.abs(q_gain)) * jnp.max(jnp.abs(k_norm_g[layer]))
        attn = lax.cond(
            score_bound <= MAX_UNSHIFTED_SCORE,
            functools.partial(_attention, unshifted=True, **attn_args),
            functools.partial(_attention, unshifted=False, **attn_args),
            lam.reshape(1).astype(F32), proj, (subln_g[layer] * (1.0 - lambda_init))[None])

        w_s = jnp.where(sgu_mask[None], gmlp_w_s[layer], 0.0).astype(BF16)
        b_s = jnp.repeat(gmlp_b_s[layer].T, w_gmlp // gheads, axis=1)
        h, xn = _outproj(
            proj, y_tb, attn, h, ssm_d[layer][None], ssm_w_glu[layer].astype(BF16),
            ssm_b_glu[layer][None], gmlp_v_g[layer][None], w_s, b_s,
            w_out[layer].astype(BF16), ffn_norm_g[layer][None], seq=seq, w_ssm=w_ssm,
            w_gmlp=w_gmlp)
        h = _ffn(xn, h, w_up[layer].astype(BF16), conv_w[layer], conv_b[layer][None],
                 w_down[layer].astype(BF16), seq=seq)
    return h.reshape(batch, seq, d)
```

```python
import functools
import math

import jax
import jax.numpy as jnp
from jax import lax
from jax.experimental import pallas as pl
from jax.experimental.pallas import tpu as pltpu

BF16 = jnp.bfloat16
F32 = jnp.float32

EPS = 1e-6
ROPE_THETA = 10000.0
CHUNK = 64
LANES = 128
SUBLANES = 8
MASKED = -1e30
VMEM_LIMIT = 56 * 1024 * 1024
MAX_UNSHIFTED_SCORE = 60.0

TS_SSM = 256
TM_PROJ = 1024
SUB_PROJ = 512
TM_FFN = 1024
TF_FFN = 1408
GELU_C = math.sqrt(2.0 / math.pi)
GELU_K = 0.044715
TQ_ATTN = 512
TK_ATTN = 256
TQ_UNSHIFTED = 1024
TK_UNSHIFTED = 256


def _rms(x, g):
    return x * lax.rsqrt(jnp.mean(x * x, axis=-1, keepdims=True) + EPS) * g


def _params(*sem, flags=None):
    return pltpu.CompilerParams(dimension_semantics=sem, vmem_limit_bytes=VMEM_LIMIT, flags=flags)


def _inproj_kernel(h_ref, g_ref, w_ref, cos_ref, sin_ref, qg_ref, kg_ref, ones_ref,
                   o_ref, u_ref, *, o1, o3, o4, o5, slab, head_dim, sub):
    lane = lax.broadcasted_iota(jnp.int32, (sub, slab), 1)
    first_half = (lane & (head_dim - 1)) < head_dim // 2
    for r in range(0, h_ref.shape[0], sub):
        rows = slice(r, r + sub)
        xn = _rms(h_ref[rows, :], g_ref[...]).astype(BF16)
        front = jnp.dot(xn, w_ref[:, 0:o3], preferred_element_type=F32).astype(BF16)
        o_ref[rows, 0:o3] = front
        u_ref[rows, :] = front[:, 0:o1]
        o_ref[rows, o5:] = jnp.dot(xn, w_ref[:, o5:], preferred_element_type=F32).astype(BF16)
        cos = cos_ref[rows, :]
        sin = sin_ref[rows, :]
        for start, stop, gain_ref in ((o3, o4, qg_ref), (o4, o5, kg_ref)):
            for c in range(start, stop, slab):
                y = jnp.dot(xn, w_ref[:, c:c + slab], preferred_element_type=F32)
                y2 = (y * y).astype(BF16)
                piece = ones_ref.shape[0]
                ss = jnp.concatenate(
                    [jnp.dot(y2[:, q:q + piece], ones_ref[...], preferred_element_type=F32)
                     for q in range(0, slab, piece)], axis=1)
                y = y * lax.rsqrt(ss * (1.0 / head_dim) + EPS) * gain_ref[...]
                rot = jnp.where(first_half,
                                pltpu.roll(y, slab - head_dim // 2, 1),
                                pltpu.roll(y, head_dim // 2, 1))
                o_ref[rows, c:c + slab] = (y * cos + rot * sin).astype(BF16)


def _inproj(h, g, w, cos_t, sin_t, qg, kg, ones, *, batch, seq, o1, o3, o4, o5, head_dim):
    n, d = h.shape
    cols = w.shape[1]
    tm = TM_PROJ
    slab = cos_t.shape[1]
    tps = seq // tm
    kern = functools.partial(_inproj_kernel, o1=o1, o3=o3, o4=o4, o5=o5, slab=slab,
                             head_dim=head_dim, sub=SUB_PROJ)
    return pl.pallas_call(
        kern,
        out_shape=(jax.ShapeDtypeStruct((n, cols), BF16),
                   jax.ShapeDtypeStruct((seq, batch * o1), BF16)),
        grid=(n // tm,),
        in_specs=[
            pl.BlockSpec((tm, d), lambda i: (i, 0)),
            pl.BlockSpec((1, d), lambda i: (0, 0)),
            pl.BlockSpec((d, cols), lambda i: (0, 0)),
            pl.BlockSpec((tm, slab), lambda i: (i % tps, 0)),
            pl.BlockSpec((tm, slab), lambda i: (i % tps, 0)),
            pl.BlockSpec((1, slab), lambda i: (0, 0)),
            pl.BlockSpec((1, slab), lambda i: (0, 0)),
            pl.BlockSpec(ones.shape, lambda i: (0, 0)),
        ],
        out_specs=(pl.BlockSpec((tm, cols), lambda i: (i, 0)),
                   pl.BlockSpec((tm, o1), lambda i: (i % tps, i // tps))),
        compiler_params=_params("parallel"),
        name="inproj",
    )(h, g, w, cos_t, sin_t, qg, kg, ones)


def _s5_kernel(u_ref, bt_ref, ct_ref, a_ref, y_ref, z_ref, x_ref, *, ts, bsz, ns):
    @pl.when(pl.program_id(0) == 0)
    def _():
        x_ref[...] = jnp.zeros(x_ref.shape, F32)

    half = ts * bsz // 2
    for r in (0, half):
        z_ref[r:r + half, :] = jnp.dot(u_ref[r:r + half, :], bt_ref[...],
                                       preferred_element_type=F32)
    ar = jnp.broadcast_to(a_ref[0:1, :], (bsz, ns))
    ai = jnp.broadcast_to(a_ref[1:2, :], (bsz, ns))

    def body(t, carry):
        xr, xi = carry
        r = pl.multiple_of(t * bsz, bsz)
        nr = ar * xr - ai * xi + z_ref[pl.ds(r, bsz), 0:ns]
        ni = ar * xi + ai * xr + z_ref[pl.ds(r, bsz), ns:]
        z_ref[pl.ds(r, bsz), 0:ns] = nr
        z_ref[pl.ds(r, bsz), ns:] = ni
        return nr, ni

    xr, xi = lax.fori_loop(0, ts, body, (x_ref[:, 0:ns], x_ref[:, ns:]), unroll=4)
    x_ref[:, 0:ns] = xr
    x_ref[:, ns:] = xi
    for r in (0, half):
        y_ref[r:r + half, :] = jnp.dot(z_ref[r:r + half, :].astype(BF16), ct_ref[...],
                                       preferred_element_type=F32).astype(BF16)


def _s5(u_tb, bt, ct, a, *, bsz):
    rows, width = u_tb.shape
    ns = a.shape[1]
    ts = TS_SSM
    kern = functools.partial(_s5_kernel, ts=ts, bsz=bsz, ns=ns)
    return pl.pallas_call(
        kern,
        out_shape=jax.ShapeDtypeStruct((rows, width), BF16),
        grid=(rows // (ts * bsz),),
        in_specs=[
            pl.BlockSpec((ts * bsz, width), lambda i: (i, 0)),
            pl.BlockSpec((width, 2 * ns), lambda i: (0, 0)),
            pl.BlockSpec((2 * ns, width), lambda i: (0, 0)),
            pl.BlockSpec((2, ns), lambda i: (0, 0)),
        ],
        out_specs=pl.BlockSpec((ts * bsz, width), lambda i: (i, 0)),
        scratch_shapes=[pltpu.VMEM((ts * bsz, 2 * ns), F32), pltpu.VMEM((bsz, 2 * ns), F32)],
        compiler_params=_params("arbitrary"),
        name="s5",
    )(u_tb, bt, ct, a)


def _s5_operators(a_re, a_im, log_dt, b_re, b_im, c_re, c_im):
    nl, ng, p = a_re.shape
    c = b_re.shape[-1]
    dt = jnp.exp(log_dt)[..., None]
    mag = jnp.exp(a_re * dt)
    ar = mag * jnp.cos(a_im * dt)
    ai = mag * jnp.sin(a_im * dt)
    den = a_re * a_re + a_im * a_im
    kr = ((ar - 1.0) * a_re + ai * a_im) / den
    ki = (ai * a_re - (ar - 1.0) * a_im) / den
    bbr = kr[..., None] * b_re - ki[..., None] * b_im
    bbi = kr[..., None] * b_im + ki[..., None] * b_re
    eye = jnp.eye(ng, dtype=F32)

    def expand_in(b):
        return jnp.einsum("lgpc,gh->lgchp", b, eye).reshape(nl, ng * c, ng * p)

    def expand_out(m):
        return jnp.einsum("lgcp,gh->lgphc", m, eye).reshape(nl, ng * p, ng * c)

    bt = jnp.concatenate([expand_in(bbr), expand_in(bbi)], axis=-1)
    ct = jnp.concatenate([expand_out(c_re), -expand_out(c_im)], axis=1)
    a = jnp.stack([ar.reshape(nl, ng * p), ai.reshape(nl, ng * p)], axis=1)
    return bt.astype(BF16), ct.astype(BF16), a


def _split_maps(q, half):
    lane = lax.broadcasted_iota(jnp.int32, q.shape, 1)
    zero = jnp.zeros_like(q)
    return jnp.where(lane < half, q, zero), jnp.where(lane >= half, q, zero)


def _chunk_mask(rows, cols, r0, c0):
    shift = CHUNK.bit_length() - 1
    row = lax.broadcasted_iota(jnp.int32, (rows, cols), 0) + r0
    col = lax.broadcasted_iota(jnp.int32, (rows, cols), 1) + c0
    return (col >> shift) <= (row >> shift)


def _attn_finish(acc0, l0, acc1, l1, lam, g_ref, o_ref):
    o = acc0 / l0 - lam * (acc1 / l1)
    o_ref[...] = _rms(o, g_ref[...]).astype(BF16)


def _attn_kernel(lam_ref, q_ref, k_ref, v_ref, g_ref, o_ref, m_ref, l_ref, acc_ref,
                 *, tq, tk, half):
    qi = pl.program_id(2)
    qs = _split_maps(q_ref[...], half)
    m_ref[...] = jnp.full(m_ref.shape, MASKED, F32)
    l_ref[...] = jnp.zeros(l_ref.shape, F32)
    acc_ref[...] = jnp.zeros(acc_ref.shape, F32)

    def block(r0, kstart, mask):
        kb = k_ref[pl.ds(kstart, tk), :]
        vb = v_ref[pl.ds(kstart, tk), :]
        for c in range(2):
            s = lax.dot_general(qs[c][r0:], kb, (((1,), (1,)), ((), ())),
                                preferred_element_type=F32)
            if mask is not None:
                s = jnp.where(mask, s, MASKED)
            m_prev = m_ref[c, r0:, :]
            m_new = jnp.maximum(m_prev, jnp.max(s, axis=1, keepdims=True))
            alpha = jnp.exp2(m_prev - m_new)
            p = jnp.exp2(s - jnp.concatenate([m_new] * (tk // LANES), axis=1))
            l_ref[c, r0:, :] = alpha * l_ref[c, r0:, :] + jnp.sum(p, axis=1, keepdims=True)
            acc_ref[c, r0:, :] = alpha * acc_ref[c, r0:, :] + jnp.dot(
                p.astype(BF16), vb, preferred_element_type=F32)
            m_ref[c, r0:, :] = m_new

    def body(j, carry):
        block(0, pl.multiple_of(j * tk, tk), None)
        return carry

    lax.fori_loop(0, qi * (tq // tk), body, 0)
    for jj in range(tq // tk):
        r0 = jj * tk
        block(r0, pl.multiple_of(qi * tq + jj * tk, tk), _chunk_mask(tq - r0, tk, r0, jj * tk))
    _attn_finish(acc_ref[0], l_ref[0], acc_ref[1], l_ref[1], lam_ref[0], g_ref, o_ref)


def _attn_unshifted_kernel(lam_ref, q_ref, k_ref, v_ref, g_ref, o_ref, l_ref, acc_ref,
                           *, tq, tk, half):
    qi = pl.program_id(2)
    qs = _split_maps(q_ref[...], half)

    def clear():
        l_ref[...] = jnp.zeros(l_ref.shape, F32)
        acc_ref[...] = jnp.zeros(acc_ref.shape, F32)

    pl.when((pl.program_id(0) == 0) & (pl.program_id(1) == 0) & (qi == 0))(clear)

    def block(r0, rows, kstart, ksize, mask):
        kb = k_ref[pl.ds(kstart, ksize), :]
        vb = v_ref[pl.ds(kstart, ksize), :]
        for c in range(2):
            s = lax.dot_general(qs[c][r0:r0 + rows], kb, (((1,), (1,)), ((), ())),
                                preferred_element_type=F32)
            p = jnp.exp2(s)
            if mask is not None:
                p = jnp.where(mask, p, 0.0)
            part = p[:, 0:LANES]
            for k in range(1, ksize // LANES):
                part = part + p[:, k * LANES:(k + 1) * LANES]
            l_ref[c, r0:r0 + rows, :] += part
            acc_ref[c, r0:r0 + rows, :] += jnp.dot(p.astype(BF16), vb,
                                                   preferred_element_type=F32)

    def body(j, carry):
        block(0, tq, pl.multiple_of(j * tq, tq), tq, None)
        return carry

    lax.fori_loop(0, qi, body, 0)
    for jj in range(tq // tk):
        r0 = jj * tk
        block(r0, tq - r0, pl.multiple_of(qi * tq + jj * tk, tk), tk,
              _chunk_mask(tq - r0, tk, r0, jj * tk))
    l0 = jnp.sum(l_ref[0], axis=1, keepdims=True)
    l1 = jnp.sum(l_ref[1], axis=1, keepdims=True)
    _attn_finish(acc_ref[0], l0, acc_ref[1], l1, lam_ref[0], g_ref, o_ref)
    clear()


def _attention(lam, proj, sub_g, *, unshifted, batch, seq, heads, vdim, qcol, kcol, vcol):
    tq, tk = (TQ_UNSHIFTED, TK_UNSHIFTED) if unshifted else (TQ_ATTN, TK_ATTN)
    nq = seq // tq
    acc = pltpu.VMEM((2, tq, vdim), F32)
    stat = pltpu.VMEM((2, tq, LANES), F32)
    if unshifted:
        kern = functools.partial(_attn_unshifted_kernel, tq=tq, tk=tk, half=vdim // 2)
        scratch = [stat, acc]
        name = "diffattn_unshifted"
    else:
        kern = functools.partial(_attn_kernel, tq=tq, tk=tk, half=vdim // 2)
        scratch = [stat, stat, acc]
        name = "diffattn"
    return pl.pallas_call(
        kern,
        out_shape=jax.ShapeDtypeStruct((batch * seq, heads * vdim), BF16),
        grid=(batch, heads, nq),
        in_specs=[
            pl.BlockSpec(memory_space=pltpu.SMEM),
            pl.BlockSpec((tq, vdim), lambda b, h, i: (b * nq + i, qcol + h)),
            pl.BlockSpec((seq, vdim), lambda b, h, i: (b, kcol + h)),
            pl.BlockSpec((seq, vdim), lambda b, h, i: (b, vcol + h)),
            pl.BlockSpec((1, vdim), lambda b, h, i: (0, 0)),
        ],
        out_specs=pl.BlockSpec((tq, vdim), lambda b, h, i: (b * nq + i, h)),
        scratch_shapes=scratch,
        compiler_params=_params("arbitrary", "arbitrary", "arbitrary"),
        name=name,
    )(lam, proj, proj, proj, sub_g)


def _outproj_kernel(p_ref, y_ref, a_ref, h_ref, d_ref, wglu_ref, bglu_ref, vgain_ref, ws_ref,
                    bs_ref, wo_ref, g2_ref, ho_ref, xn_ref, sg_ref,
                    *, w_ssm, w_gmlp, win, heads, sub):
    head_shift = (w_gmlp // heads).bit_length() - 1
    head_of_lane = lax.broadcasted_iota(jnp.int32, (win, w_gmlp), 1) >> head_shift
    for r in range(0, h_ref.shape[0], sub):
        rows = slice(r, r + sub)
        p = p_ref[rows, :]
        u = p[:, 0:w_ssm].astype(F32)
        ug = p[:, w_ssm:w_ssm + w_gmlp].astype(F32)
        vg = p[:, w_ssm + w_gmlp:].astype(F32)

        y = jax.nn.gelu(y_ref[rows, :].astype(F32) + d_ref[...] * u)
        z = jnp.dot(y.astype(BF16), wglu_ref[...], preferred_element_type=F32) + bglu_ref[...]
        y_ssm = (y * jax.nn.sigmoid(z)).astype(BF16)

        v = _rms(vg, vgain_ref[...])
        for w in range(sub // win):
            vw = v[w * win:(w + 1) * win, :]
            mixed = bs_ref[...]
            for hh in range(heads):
                vh = jnp.where(head_of_lane == hh, vw, 0.0).astype(BF16)
                mixed = mixed + jnp.dot(ws_ref[hh], vh, preferred_element_type=F32)
            sg_ref[r + w * win:r + (w + 1) * win, :] = (
                ug[w * win:(w + 1) * win, :] * mixed).astype(BF16)

        out = jnp.dot(y_ssm, wo_ref[0:w_ssm, :], preferred_element_type=F32)
        out = out + jnp.dot(sg_ref[rows, :], wo_ref[w_ssm:w_ssm + w_gmlp, :],
                            preferred_element_type=F32)
        out = out + jnp.dot(a_ref[rows, :], wo_ref[w_ssm + w_gmlp:, :],
                            preferred_element_type=F32)
        h_new = h_ref[rows, :] + out
        ho_ref[rows, :] = h_new
        xn_ref[rows, :] = _rms(h_new, g2_ref[...]).astype(BF16)


def _outproj(proj, y_tb, attn, h, d_skip, w_glu, b_glu, v_gain, w_s, b_s, w_out, g2,
             *, seq, w_ssm, w_gmlp):
    n, d = h.shape
    tm = TM_PROJ
    tps = seq // tm
    heads, win, _ = w_s.shape
    w_diff = attn.shape[1]
    front = w_ssm + 2 * w_gmlp
    assert (w_gmlp // heads) & (w_gmlp // heads - 1) == 0
    kern = functools.partial(_outproj_kernel, w_ssm=w_ssm, w_gmlp=w_gmlp, win=win, heads=heads,
                             sub=SUB_PROJ)
    const2 = lambda i: (0, 0)
    return pl.pallas_call(
        kern,
        out_shape=(jax.ShapeDtypeStruct((n, d), F32), jax.ShapeDtypeStruct((n, d), BF16)),
        grid=(n // tm,),
        in_specs=[
            pl.BlockSpec((tm, front), lambda i: (i, 0)),
            pl.BlockSpec((tm, w_ssm), lambda i: (i % tps, i // tps)),
            pl.BlockSpec((tm, w_diff), lambda i: (i, 0)),
            pl.BlockSpec((tm, d), lambda i: (i, 0)),
            pl.BlockSpec((1, w_ssm), const2),
            pl.BlockSpec((w_ssm, w_ssm), const2),
            pl.BlockSpec((1, w_ssm), const2),
            pl.BlockSpec((1, w_gmlp), const2),
            pl.BlockSpec((heads, win, win), lambda i: (0, 0, 0)),
            pl.BlockSpec((win, w_gmlp), const2),
            pl.BlockSpec((w_ssm + w_gmlp + w_diff, d), const2),
            pl.BlockSpec((1, d), const2),
        ],
        out_specs=(pl.BlockSpec((tm, d), lambda i: (i, 0)), pl.BlockSpec((tm, d), lambda i: (i, 0))),
        scratch_shapes=[pltpu.VMEM((tm, w_gmlp), BF16)],
        compiler_params=_params("parallel"),
        name="outproj",
    )(proj, y_tb, attn, h, d_skip, w_glu, b_glu, v_gain, w_s, b_s, w_out, g2)


def _ffn_kernel(x_ref, halo_ref, h_ref, wg_ref, wv_ref, cwg_ref, cwv_ref, cbg_ref, cbv_ref,
                wd_ref, o_ref, sg_ref, sv_ref, *, tm, tiles_per_seq, taps):
    i = pl.program_id(0)

    @pl.when(pl.program_id(1) == 0)
    def _():
        o_ref[...] = h_ref[...]

    halo = halo_ref[...]
    halo = jnp.where(i % tiles_per_seq == 0, jnp.zeros_like(halo), halo)
    x = x_ref[...]

    def conv(w_ref, s_ref, cw_ref, cb_ref):
        s_ref[0:SUBLANES, :] = jnp.dot(halo, w_ref[...], preferred_element_type=F32)
        s_ref[SUBLANES:, :] = jnp.dot(x, w_ref[...], preferred_element_type=F32)
        full = s_ref[...]
        acc = cb_ref[...] + cw_ref[taps - 1:taps, :] * full[SUBLANES:, :]
        for back in range(1, taps):
            shifted = pltpu.roll(full, back, 0)[SUBLANES:, :]
            acc = acc + cw_ref[taps - 1 - back:taps - back, :] * shifted
        return acc

    gate = conv(wg_ref, sg_ref, cwg_ref, cbg_ref)
    half_val = conv(wv_ref, sv_ref, cwv_ref, cbv_ref)
    inner = gate * (GELU_C + (GELU_C * GELU_K) * (gate * gate))
    a = (gate * (1.0 + jnp.tanh(inner)) * half_val).astype(BF16)
    o_ref[...] += jnp.dot(a, wd_ref[...], preferred_element_type=F32)


def _ffn(xn, h, w_up, conv_w, conv_b, w_down, *, seq):
    n, d = h.shape
    f = w_down.shape[0]
    taps = conv_w.shape[0]
    tm, tf = TM_FFN, TF_FFN
    nf = f // tf
    tiles_per_seq = seq // tm
    kern = functools.partial(_ffn_kernel, tm=tm, tiles_per_seq=tiles_per_seq, taps=taps)
    halo_blocks = tm // SUBLANES
    weight_mode = pl.Buffered(1) if nf == 1 else None
    half = jnp.concatenate([jnp.ones((f,), F32), jnp.full((f,), 0.5, F32)])
    conv_w = conv_w * half
    conv_b = conv_b * half
    return pl.pallas_call(
        kern,
        out_shape=jax.ShapeDtypeStruct((n, d), F32),
        grid=(n // tm, nf),
        in_specs=[
            pl.BlockSpec((tm, d), lambda i, j: (i, 0)),
            pl.BlockSpec((SUBLANES, d), lambda i, j: (jnp.maximum(i * halo_blocks - 1, 0), 0)),
            pl.BlockSpec((tm, d), lambda i, j: (i, 0)),
            pl.BlockSpec((d, tf), lambda i, j: (0, j), pipeline_mode=weight_mode),
            pl.BlockSpec((d, tf), lambda i, j: (0, nf + j), pipeline_mode=weight_mode),
            pl.BlockSpec((taps, tf), lambda i, j: (0, j)),
            pl.BlockSpec((taps, tf), lambda i, j: (0, nf + j)),
            pl.BlockSpec((1, tf), lambda i, j: (0, j)),
            pl.BlockSpec((1, tf), lambda i, j: (0, nf + j)),
            pl.BlockSpec((tf, d), lambda i, j: (j, 0), pipeline_mode=weight_mode),
        ],
        out_specs=pl.BlockSpec((tm, d), lambda i, j: (i, 0)),
        scratch_shapes=[pltpu.VMEM((tm + SUBLANES, tf), F32), pltpu.VMEM((tm + SUBLANES, tf), F32)],
        compiler_params=_params("parallel", "arbitrary"),
        name="convffn",
    )(xn, xn, h, w_up, w_up, conv_w, conv_w, conv_b, conv_b, w_down)


def _rope_tables(seq, head_dim, reps):
    half = head_dim // 2
    inv = ROPE_THETA ** (-jnp.arange(half, dtype=F32) / half)
    ang = jnp.arange(seq, dtype=F32)[:, None] * inv[None, :]
    ang = jnp.concatenate([ang, ang], axis=-1)
    sign = jnp.concatenate([-jnp.ones((half,), F32), jnp.ones((half,), F32)])
    return jnp.tile(jnp.cos(ang), (1, reps)), jnp.tile(jnp.sin(ang) * sign, (1, reps))


def kernel(x, attn_norm_g, w_in, ssm_a_re, ssm_a_im, ssm_log_dt, ssm_b_re, ssm_b_im, ssm_c_re, ssm_c_im, ssm_d, ssm_w_glu, ssm_b_glu, gmlp_v_g, gmlp_w_s, gmlp_b_s, q_norm_g, k_norm_g, lambda_q1, lambda_k1, lambda_q2, lambda_k2, subln_g, w_out, ffn_norm_g, w_up, conv_w, conv_b, w_down):
    batch, seq, d = x.shape
    depth, _, in_cols = w_in.shape
    w_ssm = ssm_d.shape[1]
    w_gmlp = gmlp_v_g.shape[1]
    w_diff = (in_cols - w_ssm - 2 * w_gmlp) // 3
    qk_dim = q_norm_g.shape[1]
    vdim = subln_g.shape[1]
    heads = w_diff // vdim
    gheads, win = gmlp_w_s.shape[1], gmlp_w_s.shape[2]
    o3 = w_ssm + 2 * w_gmlp
    o4 = o3 + w_diff
    o5 = o4 + w_diff
    assert vdim == LANES and 2 * qk_dim == vdim and batch == SUBLANES

    slab = w_diff
    cos_t, sin_t = _rope_tables(seq, qk_dim, slab // qk_dim)
    lane = jnp.arange(2 * LANES)
    ones = (lane[:, None] // qk_dim == lane[None, :] // qk_dim).astype(BF16)
    q_scale = qk_dim ** -0.5 * math.log2(math.e)
    bt, ct, a_bar = _s5_operators(ssm_a_re, ssm_a_im, ssm_log_dt, ssm_b_re, ssm_b_im,
                                  ssm_c_re, ssm_c_im)
    pos_chunk = jnp.arange(win) // CHUNK
    sgu_mask = pos_chunk[None, :] <= pos_chunk[:, None]
    attn_args = dict(batch=batch, seq=seq, heads=heads, vdim=vdim,
                     qcol=o3 // vdim, kcol=o4 // vdim, vcol=o5 // vdim)

    h = x.reshape(batch * seq, d)
    for layer in range(depth):
        lambda_init = 0.8 - 0.6 * math.exp(-0.3 * layer)
        lam = (jnp.exp(jnp.sum(lambda_q1[layer] * lambda_k1[layer]))
               - jnp.exp(jnp.sum(lambda_q2[layer] * lambda_k2[layer])) + lambda_init)
        q_gain = q_norm_g[layer] * q_scale
        proj, u_tb = _inproj(
            h, attn_norm_g[layer][None], w_in[layer].astype(BF16), cos_t, sin_t,
            jnp.tile(q_gain, slab // qk_dim)[None],
            jnp.tile(k_norm_g[layer], slab // qk_dim)[None], ones,
            batch=batch, seq=seq, o1=w_ssm, o3=o3, o4=o4, o5=o5, head_dim=qk_dim)

        y_tb = _s5(u_tb.reshape(seq * batch, w_ssm), bt[layer], ct[layer], a_bar[layer],
                   bsz=batch).reshape(seq, batch * w_ssm)

        score_bound = qk_dim * jnp.max(jnp.abs(q_gain)) * jnp.max(jnp.abs(k_norm_g[layer]))
        attn = lax.cond(
            score_bound <= MAX_UNSHIFTED_SCORE,
            functools.partial(_attention, unshifted=True, **attn_args),
            functools.partial(_attention, unshifted=False, **attn_args),
            lam.reshape(1).astype(F32), proj, (subln_g[layer] * (1.0 - lambda_init))[None])

        w_s = jnp.where(sgu_mask[None], gmlp_w_s[layer], 0.0).astype(BF16)
        b_s = jnp.repeat(gmlp_b_s[layer].T, w_gmlp // gheads, axis=1)
        h, xn = _outproj(
            proj, y_tb, attn, h, ssm_d[layer][None], ssm_w_glu[layer].astype(BF16),
            ssm_b_glu[layer][None], gmlp_v_g[layer][None], w_s, b_s,
            w_out[layer].astype(BF16), ffn_norm_g[layer][None], seq=seq, w_ssm=w_ssm,
            w_gmlp=w_gmlp)
        h = _ffn(xn, h, w_up[layer].astype(BF16), conv_w[layer], conv_b[layer][None],
                 w_down[layer].astype(BF16), seq=seq)
    return h.reshape(batch, seq, d)
```

```python
import functools
import math

import jax
import jax.numpy as jnp
from jax import lax
from jax.experimental import pallas as pl
from jax.experimental.pallas import tpu as pltpu

BF16 = jnp.bfloat16
F32 = jnp.float32

EPS = 1e-6
ROPE_THETA = 10000.0
CHUNK = 64
LANES = 128
SUBLANES = 8
MASKED = -1e30
VMEM_LIMIT = 56 * 1024 * 1024
MAX_UNSHIFTED_SCORE = 60.0

TS_SSM = 256
TM_PROJ = 1024
SUB_PROJ = 512
TM_FFN = 1024
TF_FFN = 1408
GELU_C = math.sqrt(2.0 / math.pi)
GELU_K = 0.044715
TQ_ATTN = 512
TK_ATTN = 256
TQ_UNSHIFTED = 1024
TK_UNSHIFTED = 256


def _rms(x, g):
    return x * lax.rsqrt(jnp.mean(x * x, axis=-1, keepdims=True) + EPS) * g


def _params(*sem, flags=None):
    return pltpu.CompilerParams(dimension_semantics=sem, vmem_limit_bytes=VMEM_LIMIT, flags=flags)


def _inproj_kernel(h_ref, g_ref, w_ref, cos_ref, sin_ref, qg_ref, kg_ref, ones_ref,
                   o_ref, u_ref, *, o1, o3, o4, o5, slab, head_dim, sub):
    lane = lax.broadcasted_iota(jnp.int32, (sub, slab), 1)
    first_half = (lane & (head_dim - 1)) < head_dim // 2
    for r in range(0, h_ref.shape[0], sub):
        rows = slice(r, r + sub)
        xn = _rms(h_ref[rows, :], g_ref[...]).astype(BF16)
        front = jnp.dot(xn, w_ref[:, 0:o3], preferred_element_type=F32).astype(BF16)
        o_ref[rows, 0:o3] = front
        u_ref[rows, :] = front[:, 0:o1]
        o_ref[rows, o5:] = jnp.dot(xn, w_ref[:, o5:], preferred_element_type=F32).astype(BF16)
        cos = cos_ref[rows, :]
        sin = sin_ref[rows, :]
        for start, stop, gain_ref in ((o3, o4, qg_ref), (o4, o5, kg_ref)):
            for c in range(start, stop, slab):
                y = jnp.dot(xn, w_ref[:, c:c + slab], preferred_element_type=F32)
                y2 = (y * y).astype(BF16)
                piece = ones_ref.shape[0]
                ss = jnp.concatenate(
                    [jnp.dot(y2[:, q:q + piece], ones_ref[...], preferred_element_type=F32)
                     for q in range(0, slab, piece)], axis=1)
                y = y * lax.rsqrt(ss * (1.0 / head_dim) + EPS) * gain_ref[...]
                rot = jnp.where(first_half,
                                pltpu.roll(y, slab - head_dim // 2, 1),
                                pltpu.roll(y, head_dim // 2, 1))
                o_ref[rows, c:c + slab] = (y * cos + rot * sin).astype(BF16)


def _inproj(h, g, w, cos_t, sin_t, qg, kg, ones, *, batch, seq, o1, o3, o4, o5, head_dim):
    n, d = h.shape
    cols = w.shape[1]
    tm = TM_PROJ
    slab = cos_t.shape[1]
    tps = seq // tm
    kern = functools.partial(_inproj_kernel, o1=o1, o3=o3, o4=o4, o5=o5, slab=slab,
                             head_dim=head_dim, sub=SUB_PROJ)
    return pl.pallas_call(
        kern,
        out_shape=(jax.ShapeDtypeStruct((n, cols), BF16),
                   jax.ShapeDtypeStruct((seq, batch * o1), BF16)),
        grid=(n // tm,),
        in_specs=[
            pl.BlockSpec((tm, d), lambda i: (i, 0)),
            pl.BlockSpec((1, d), lambda i: (0, 0)),
            pl.BlockSpec((d, cols), lambda i: (0, 0)),
            pl.BlockSpec((tm, slab), lambda i: (i % tps, 0)),
            pl.BlockSpec((tm, slab), lambda i: (i % tps, 0)),
            pl.BlockSpec((1, slab), lambda i: (0, 0)),
            pl.BlockSpec((1, slab), lambda i: (0, 0)),
            pl.BlockSpec(ones.shape, lambda i: (0, 0)),
        ],
        out_specs=(pl.BlockSpec((tm, cols), lambda i: (i, 0)),
                   pl.BlockSpec((tm, o1), lambda i: (i % tps, i // tps))),
        compiler_params=_params("parallel"),
        name="inproj",
    )(h, g, w, cos_t, sin_t, qg, kg, ones)


def _s5_kernel(u_ref, bt_ref, ct_ref, a_ref, y_ref, z_ref, x_ref, ub_ref, yb_ref,
               *, ts, bsz, ns):
    @pl.when(pl.program_id(0) == 0)
    def _():
        x_ref[...] = jnp.zeros(x_ref.shape, F32)

    width = u_ref.shape[1] // bsz
    slabs = width // LANES
    for b in range(bsz):
        ub = u_ref[:, b * width:(b + 1) * width].astype(F32)
        for s in range(slabs):
            ub_ref[s, pl.ds(b, ts, stride=bsz), :] = ub[:, s * LANES:(s + 1) * LANES]
    u = jnp.concatenate([ub_ref[s] for s in range(slabs)], axis=1).astype(BF16)

    half = ts * bsz // 2
    for r in (0, half):
        z_ref[r:r + half, :] = jnp.dot(u[r:r + half, :], bt_ref[...],
                                       preferred_element_type=F32)
    ar = jnp.broadcast_to(a_ref[0:1, :], (bsz, ns))
    ai = jnp.broadcast_to(a_ref[1:2, :], (bsz, ns))

    def body(t, carry):
        xr, xi = carry
        r = pl.multiple_of(t * bsz, bsz)
        nr = ar * xr - ai * xi + z_ref[pl.ds(r, bsz), 0:ns]
        ni = ar * xi + ai * xr + z_ref[pl.ds(r, bsz), ns:]
        z_ref[pl.ds(r, bsz), 0:ns] = nr
        z_ref[pl.ds(r, bsz), ns:] = ni
        return nr, ni

    xr, xi = lax.fori_loop(0, ts, body, (x_ref[:, 0:ns], x_ref[:, ns:]), unroll=4)
    x_ref[:, 0:ns] = xr
    x_ref[:, ns:] = xi
    for r in (0, half):
        yv = jnp.dot(z_ref[r:r + half, :].astype(BF16), ct_ref[...], preferred_element_type=F32)
        for s in range(slabs):
            yb_ref[s, r:r + half, :] = yv[:, s * LANES:(s + 1) * LANES]
    for b in range(bsz):
        for s in range(slabs):
            lanes = slice(b * width + s * LANES, b * width + (s + 1) * LANES)
            y_ref[:, lanes] = yb_ref[s, pl.ds(b, ts, stride=bsz), :].astype(BF16)


def _s5(u_tb, bt, ct, a, *, bsz):
    seq, cols = u_tb.shape
    width = cols // bsz
    ns = a.shape[1]
    ts = TS_SSM
    kern = functools.partial(_s5_kernel, ts=ts, bsz=bsz, ns=ns)
    slab = pltpu.VMEM((width // LANES, ts * bsz, LANES), F32)
    return pl.pallas_call(
        kern,
        out_shape=jax.ShapeDtypeStruct((seq, cols), BF16),
        grid=(seq // ts,),
        in_specs=[
            pl.BlockSpec((ts, cols), lambda i: (i, 0)),
            pl.BlockSpec((width, 2 * ns), lambda i: (0, 0)),
            pl.BlockSpec((2 * ns, width), lambda i: (0, 0)),
            pl.BlockSpec((2, ns), lambda i: (0, 0)),
        ],
        out_specs=pl.BlockSpec((ts, cols), lambda i: (i, 0)),
        scratch_shapes=[pltpu.VMEM((ts * bsz, 2 * ns), F32), pltpu.VMEM((bsz, 2 * ns), F32),
                        slab, slab],
        compiler_params=_params("arbitrary"),
        name="s5",
    )(u_tb, bt, ct, a)


def _s5_operators(a_re, a_im, log_dt, b_re, b_im, c_re, c_im):
    nl, ng, p = a_re.shape
    c = b_re.shape[-1]
    dt = jnp.exp(log_dt)[..., None]
    mag = jnp.exp(a_re * dt)
    ar = mag * jnp.cos(a_im * dt)
    ai = mag * jnp.sin(a_im * dt)
    den = a_re * a_re + a_im * a_im
    kr = ((ar - 1.0) * a_re + ai * a_im) / den
    ki = (ai * a_re - (ar - 1.0) * a_im) / den
    bbr = kr[..., None] * b_re - ki[..., None] * b_im
    bbi = kr[..., None] * b_im + ki[..., None] * b_re
    eye = jnp.eye(ng, dtype=F32)

    def expand_in(b):
        return jnp.einsum("lgpc,gh->lgchp", b, eye).reshape(nl, ng * c, ng * p)

    def expand_out(m):
        return jnp.einsum("lgcp,gh->lgphc", m, eye).reshape(nl, ng * p, ng * c)

    bt = jnp.concatenate([expand_in(bbr), expand_in(bbi)], axis=-1)
    ct = jnp.concatenate([expand_out(c_re), -expand_out(c_im)], axis=1)
    a = jnp.stack([ar.reshape(nl, ng * p), ai.reshape(nl, ng * p)], axis=1)
    return bt.astype(BF16), ct.astype(BF16), a


def _split_maps(q, half):
    lane = lax.broadcasted_iota(jnp.int32, q.shape, 1)
    zero = jnp.zeros_like(q)
    return jnp.where(lane < half, q, zero), jnp.where(lane >= half, q, zero)


def _chunk_mask(rows, cols, r0, c0):
    shift = CHUNK.bit_length() - 1
    row = lax.broadcasted_iota(jnp.int32, (rows, cols), 0) + r0
    col = lax.broadcasted_iota(jnp.int32, (rows, cols), 1) + c0
    return (col >> shift) <= (row >> shift)


def _attn_finish(acc0, l0, acc1, l1, lam, g_ref, o_ref):
    o = acc0 / l0 - lam * (acc1 / l1)
    o_ref[...] = _rms(o, g_ref[...]).astype(BF16)


def _attn_kernel(lam_ref, q_ref, k_ref, v_ref, g_ref, o_ref, m_ref, l_ref, acc_ref,
                 *, tq, tk, half):
    qi = pl.program_id(2)
    qs = _split_maps(q_ref[...], half)
    m_ref[...] = jnp.full(m_ref.shape, MASKED, F32)
    l_ref[...] = jnp.zeros(l_ref.shape, F32)
    acc_ref[...] = jnp.zeros(acc_ref.shape, F32)

    def block(r0, kstart, mask):
        kb = k_ref[pl.ds(kstart, tk), :]
        vb = v_ref[pl.ds(kstart, tk), :]
        for c in range(2):
            s = lax.dot_general(qs[c][r0:], kb, (((1,), (1,)), ((), ())),
                                preferred_element_type=F32)
            if mask is not None:
                s = jnp.where(mask, s, MASKED)
            m_prev = m_ref[c, r0:, :]
            m_new = jnp.maximum(m_prev, jnp.max(s, axis=1, keepdims=True))
            alpha = jnp.exp2(m_prev - m_new)
            p = jnp.exp2(s - jnp.concatenate([m_new] * (tk // LANES), axis=1))
            l_ref[c, r0:, :] = alpha * l_ref[c, r0:, :] + jnp.sum(p, axis=1, keepdims=True)
            acc_ref[c, r0:, :] = alpha * acc_ref[c, r0:, :] + jnp.dot(
                p.astype(BF16), vb, preferred_element_type=F32)
            m_ref[c, r0:, :] = m_new

    def body(j, carry):
        block(0, pl.multiple_of(j * tk, tk), None)
        return carry

    lax.fori_loop(0, qi * (tq // tk), body, 0)
    for jj in range(tq // tk):
        r0 = jj * tk
        block(r0, pl.multiple_of(qi * tq + jj * tk, tk), _chunk_mask(tq - r0, tk, r0, jj * tk))
    _attn_finish(acc_ref[0], l_ref[0], acc_ref[1], l_ref[1], lam_ref[0], g_ref, o_ref)


def _attn_unshifted_kernel(lam_ref, q_ref, k_ref, v_ref, g_ref, o_ref, l_ref, acc_ref,
                           *, tq, tk, half):
    qi = pl.program_id(2)
    qs = _split_maps(q_ref[...], half)

    def clear():
        l_ref[...] = jnp.zeros(l_ref.shape, F32)
        acc_ref[...] = jnp.zeros(acc_ref.shape, F32)

    pl.when((pl.program_id(0) == 0) & (pl.program_id(1) == 0) & (qi == 0))(clear)

    def block(r0, rows, kstart, ksize, mask):
        kb = k_ref[pl.ds(kstart, ksize), :]
        vb = v_ref[pl.ds(kstart, ksize), :]
        for c in range(2):
            s = lax.dot_general(qs[c][r0:r0 + rows], kb, (((1,), (1,)), ((), ())),
                                preferred_element_type=F32)
            p = jnp.exp2(s)
            if mask is not None:
                p = jnp.where(mask, p, 0.0)
            part = p[:, 0:LANES]
            for k in range(1, ksize // LANES):
                part = part + p[:, k * LANES:(k + 1) * LANES]
            l_ref[c, r0:r0 + rows, :] += part
            acc_ref[c, r0:r0 + rows, :] += jnp.dot(p.astype(BF16), vb,
                                                   preferred_element_type=F32)

    def body(j, carry):
        block(0, tq, pl.multiple_of(j * tq, tq), tq, None)
        return carry

    lax.fori_loop(0, qi, body, 0)
    for jj in range(tq // tk):
        r0 = jj * tk
        block(r0, tq - r0, pl.multiple_of(qi * tq + jj * tk, tk), tk,
              _chunk_mask(tq - r0, tk, r0, jj * tk))
    l0 = jnp.sum(l_ref[0], axis=1, keepdims=True)
    l1 = jnp.sum(l_ref[1], axis=1, keepdims=True)
    _attn_finish(acc_ref[0], l0, acc_ref[1], l1, lam_ref[0], g_ref, o_ref)
    clear()


def _attention(lam, proj, sub_g, *, unshifted, batch, seq, heads, vdim, qcol, kcol, vcol):
    tq, tk = (TQ_UNSHIFTED, TK_UNSHIFTED) if unshifted else (TQ_ATTN, TK_ATTN)
    nq = seq // tq
    acc = pltpu.VMEM((2, tq, vdim), F32)
    stat = pltpu.VMEM((2, tq, LANES), F32)
    if unshifted:
        kern = functools.partial(_attn_unshifted_kernel, tq=tq, tk=tk, half=vdim // 2)
        scratch = [stat, acc]
        name = "diffattn_unshifted"
    else:
        kern = functools.partial(_attn_kernel, tq=tq, tk=tk, half=vdim // 2)
        scratch = [stat, stat, acc]
        name = "diffattn"
    return pl.pallas_call(
        kern,
        out_shape=jax.ShapeDtypeStruct((batch * seq, heads * vdim), BF16),
        grid=(batch, heads, nq),
        in_specs=[
            pl.BlockSpec(memory_space=pltpu.SMEM),
            pl.BlockSpec((tq, vdim), lambda b, h, i: (b * nq + i, qcol + h)),
            pl.BlockSpec((seq, vdim), lambda b, h, i: (b, kcol + h)),
            pl.BlockSpec((seq, vdim), lambda b, h, i: (b, vcol + h)),
            pl.BlockSpec((1, vdim), lambda b, h, i: (0, 0)),
        ],
        out_specs=pl.BlockSpec((tq, vdim), lambda b, h, i: (b * nq + i, h)),
        scratch_shapes=scratch,
        compiler_params=_params("arbitrary", "arbitrary", "arbitrary"),
        name=name,
    )(lam, proj, proj, proj, sub_g)


def _outproj_kernel(p_ref, y_ref, a_ref, h_ref, d_ref, wglu_ref, bglu_ref, vgain_ref, ws_ref,
                    bs_ref, wo_ref, g2_ref, ho_ref, xn_ref, sg_ref,
                    *, w_ssm, w_gmlp, win, heads, sub):
    head_shift = (w_gmlp // heads).bit_length() - 1
    head_of_lane = lax.broadcasted_iota(jnp.int32, (win, w_gmlp), 1) >> head_shift
    for r in range(0, h_ref.shape[0], sub):
        rows = slice(r, r + sub)
        p = p_ref[rows, :]
        u = p[:, 0:w_ssm].astype(F32)
        ug = p[:, w_ssm:w_ssm + w_gmlp].astype(F32)
        vg = p[:, w_ssm + w_gmlp:].astype(F32)

        y = jax.nn.gelu(y_ref[rows, :].astype(F32) + d_ref[...] * u)
        z = jnp.dot(y.astype(BF16), wglu_ref[...], preferred_element_type=F32) + bglu_ref[...]
        y_ssm = (y * jax.nn.sigmoid(z)).astype(BF16)

        v = _rms(vg, vgain_ref[...])
        for w in range(sub // win):
            vw = v[w * win:(w + 1) * win, :]
            mixed = bs_ref[...]
            for hh in range(heads):
                vh = jnp.where(head_of_lane == hh, vw, 0.0).astype(BF16)
                mixed = mixed + jnp.dot(ws_ref[hh], vh, preferred_element_type=F32)
            sg_ref[r + w * win:r + (w + 1) * win, :] = (
                ug[w * win:(w + 1) * win, :] * mixed).astype(BF16)

        out = jnp.dot(y_ssm, wo_ref[0:w_ssm, :], preferred_element_type=F32)
        out = out + jnp.dot(sg_ref[rows, :], wo_ref[w_ssm:w_ssm + w_gmlp, :],
                            preferred_element_type=F32)
        out = out + jnp.dot(a_ref[rows, :], wo_ref[w_ssm + w_gmlp:, :],
                            preferred_element_type=F32)
        h_new = h_ref[rows, :] + out
        ho_ref[rows, :] = h_new
        xn_ref[rows, :] = _rms(h_new, g2_ref[...]).astype(BF16)


def _outproj(proj, y_tb, attn, h, d_skip, w_glu, b_glu, v_gain, w_s, b_s, w_out, g2,
             *, seq, w_ssm, w_gmlp):
    n, d = h.shape
    tm = TM_PROJ
    tps = seq // tm
    heads, win, _ = w_s.shape
    w_diff = attn.shape[1]
    front = w_ssm + 2 * w_gmlp
    assert (w_gmlp // heads) & (w_gmlp // heads - 1) == 0
    kern = functools.partial(_outproj_kernel, w_ssm=w_ssm, w_gmlp=w_gmlp, win=win, heads=heads,
                             sub=SUB_PROJ)
    const2 = lambda i: (0, 0)
    return pl.pallas_call(
        kern,
        out_shape=(jax.ShapeDtypeStruct((n, d), F32), jax.ShapeDtypeStruct((n, d), BF16)),
        grid=(n // tm,),
        in_specs=[
            pl.BlockSpec((tm, front), lambda i: (i, 0)),
            pl.BlockSpec((tm, w_ssm), lambda i: (i % tps, i // tps)),
            pl.BlockSpec((tm, w_diff), lambda i: (i, 0)),
            pl.BlockSpec((tm, d), lambda i: (i, 0)),
            pl.BlockSpec((1, w_ssm), const2),
            pl.BlockSpec((w_ssm, w_ssm), const2),
            pl.BlockSpec((1, w_ssm), const2),
            pl.BlockSpec((1, w_gmlp), const2),
            pl.BlockSpec((heads, win, win), lambda i: (0, 0, 0)),
            pl.BlockSpec((win, w_gmlp), const2),
            pl.BlockSpec((w_ssm + w_gmlp + w_diff, d), const2),
            pl.BlockSpec((1, d), const2),
        ],
        out_specs=(pl.BlockSpec((tm, d), lambda i: (i, 0)), pl.BlockSpec((tm, d), lambda i: (i, 0))),
        scratch_shapes=[pltpu.VMEM((tm, w_gmlp), BF16)],
        compiler_params=_params("parallel"),
        name="outproj",
    )(proj, y_tb, attn, h, d_skip, w_glu, b_glu, v_gain, w_s, b_s, w_out, g2)


def _ffn_kernel(x_ref, halo_ref, h_ref, wg_ref, wv_ref, cwg_ref, cwv_ref, cbg_ref, cbv_ref,
                wd_ref, o_ref, sg_ref, sv_ref, *, tm, tiles_per_seq, taps):
    i = pl.program_id(0)

    @pl.when(pl.program_id(1) == 0)
    def _():
        o_ref[...] = h_ref[...]

    halo = halo_ref[...]
    halo = jnp.where(i % tiles_per_seq == 0, jnp.zeros_like(halo), halo)
    x = x_ref[...]

    def conv(w_ref, s_ref, cw_ref, cb_ref):
        s_ref[0:SUBLANES, :] = jnp.dot(halo, w_ref[...], preferred_element_type=F32)
        s_ref[SUBLANES:, :] = jnp.dot(x, w_ref[...], preferred_element_type=F32)
        full = s_ref[...]
        acc = cb_ref[...] + cw_ref[taps - 1:taps, :] * full[SUBLANES:, :]
        for back in range(1, taps):
            shifted = pltpu.roll(full, back, 0)[SUBLANES:, :]
            acc = acc + cw_ref[taps - 1 - back:taps - back, :] * shifted
        return acc

    gate = conv(wg_ref, sg_ref, cwg_ref, cbg_ref)
    half_val = conv(wv_ref, sv_ref, cwv_ref, cbv_ref)
    inner = gate * (GELU_C + (GELU_C * GELU_K) * (gate * gate))
    a = (gate * (1.0 + jnp.tanh(inner)) * half_val).astype(BF16)
    o_ref[...] += jnp.dot(a, wd_ref[...], preferred_element_type=F32)


def _ffn(xn, h, w_up, conv_w, conv_b, w_down, *, seq):
    n, d = h.shape
    f = w_down.shape[0]
    taps = conv_w.shape[0]
    tm, tf = TM_FFN, TF_FFN
    nf = f // tf
    tiles_per_seq = seq // tm
    kern = functools.partial(_ffn_kernel, tm=tm, tiles_per_seq=tiles_per_seq, taps=taps)
    halo_blocks = tm // SUBLANES
    weight_mode = pl.Buffered(1) if nf == 1 else None
    half = jnp.concatenate([jnp.ones((f,), F32), jnp.full((f,), 0.5, F32)])
    conv_w = conv_w * half
    conv_b = conv_b * half
    return pl.pallas_call(
        kern,
        out_shape=jax.ShapeDtypeStruct((n, d), F32),
        grid=(n // tm, nf),
        in_specs=[
            pl.BlockSpec((tm, d), lambda i, j: (i, 0)),
            pl.BlockSpec((SUBLANES, d), lambda i, j: (jnp.maximum(i * halo_blocks - 1, 0), 0)),
            pl.BlockSpec((tm, d), lambda i, j: (i, 0)),
            pl.BlockSpec((d, tf), lambda i, j: (0, j), pipeline_mode=weight_mode),
            pl.BlockSpec((d, tf), lambda i, j: (0, nf + j), pipeline_mode=weight_mode),
            pl.BlockSpec((taps, tf), lambda i, j: (0, j)),
            pl.BlockSpec((taps, tf), lambda i, j: (0, nf + j)),
            pl.BlockSpec((1, tf), lambda i, j: (0, j)),
            pl.BlockSpec((1, tf), lambda i, j: (0, nf + j)),
            pl.BlockSpec((tf, d), lambda i, j: (j, 0), pipeline_mode=weight_mode),
        ],
        out_specs=pl.BlockSpec((tm, d), lambda i, j: (i, 0)),
        scratch_shapes=[pltpu.VMEM((tm + SUBLANES, tf), F32), pltpu.VMEM((tm + SUBLANES, tf), F32)],
        compiler_params=_params("parallel", "arbitrary"),
        name="convffn",
    )(xn, xn, h, w_up, w_up, conv_w, conv_w, conv_b, conv_b, w_down)


def _rope_tables(seq, head_dim, reps):
    half = head_dim // 2
    inv = ROPE_THETA ** (-jnp.arange(half, dtype=F32) / half)
    ang = jnp.arange(seq, dtype=F32)[:, None] * inv[None, :]
    ang = jnp.concatenate([ang, ang], axis=-1)
    sign = jnp.concatenate([-jnp.ones((half,), F32), jnp.ones((half,), F32)])
    return jnp.tile(jnp.cos(ang), (1, reps)), jnp.tile(jnp.sin(ang) * sign, (1, reps))


def kernel(x, attn_norm_g, w_in, ssm_a_re, ssm_a_im, ssm_log_dt, ssm_b_re, ssm_b_im, ssm_c_re, ssm_c_im, ssm_d, ssm_w_glu, ssm_b_glu, gmlp_v_g, gmlp_w_s, gmlp_b_s, q_norm_g, k_norm_g, lambda_q1, lambda_k1, lambda_q2, lambda_k2, subln_g, w_out, ffn_norm_g, w_up, conv_w, conv_b, w_down):
    batch, seq, d = x.shape
    depth, _, in_cols = w_in.shape
    w_ssm = ssm_d.shape[1]
    w_gmlp = gmlp_v_g.shape[1]
    w_diff = (in_cols - w_ssm - 2 * w_gmlp) // 3
    qk_dim = q_norm_g.shape[1]
    vdim = subln_g.shape[1]
    heads = w_diff // vdim
    gheads, win = gmlp_w_s.shape[1], gmlp_w_s.shape[2]
    o3 = w_ssm + 2 * w_gmlp
    o4 = o3 + w_diff
    o5 = o4 + w_diff
    assert vdim == LANES and 2 * qk_dim == vdim and batch == SUBLANES

    slab = w_diff
    cos_t, sin_t = _rope_tables(seq, qk_dim, slab // qk_dim)
    lane = jnp.arange(2 * LANES)
    ones = (lane[:, None] // qk_dim == lane[None, :] // qk_dim).astype(BF16)
    q_scale = qk_dim ** -0.5 * math.log2(math.e)
    bt, ct, a_bar = _s5_operators(ssm_a_re, ssm_a_im, ssm_log_dt, ssm_b_re, ssm_b_im,
                                  ssm_c_re, ssm_c_im)
    pos_chunk = jnp.arange(win) // CHUNK
    sgu_mask = pos_chunk[None, :] <= pos_chunk[:, None]
    attn_args = dict(batch=batch, seq=seq, heads=heads, vdim=vdim,
                     qcol=o3 // vdim, kcol=o4 // vdim, vcol=o5 // vdim)

    h = x.reshape(batch * seq, d)
    for layer in range(depth):
        lambda_init = 0.8 - 0.6 * math.exp(-0.3 * layer)
        lam = (jnp.exp(jnp.sum(lambda_q1[layer] * lambda_k1[layer]))
               - jnp.exp(jnp.sum(lambda_q2[layer] * lambda_k2[layer])) + lambda_init)
        q_gain = q_norm_g[layer] * q_scale
        proj, u_tb = _inproj(
            h, attn_norm_g[layer][None], w_in[layer].astype(BF16), cos_t, sin_t,
            jnp.tile(q_gain, slab // qk_dim)[None],
            jnp.tile(k_norm_g[layer], slab // qk_dim)[None], ones,
            batch=batch, seq=seq, o1=w_ssm, o3=o3, o4=o4, o5=o5, head_dim=qk_dim)

        y_tb = _s5(u_tb, bt[layer], ct[layer], a_bar[layer], bsz=batch)

        score_bound = qk_dim * jnp.max(jnp.abs(q_gain)) * jnp.max(jnp.abs(k_norm_g[layer]))
        attn = lax.cond(
            score_bound <= MAX_UNSHIFTED_SCORE,
            functools.partial(_attention, unshifted=True, **attn_args),
            functools.partial(_attention, unshifted=False, **attn_args),
            lam.reshape(1).astype(F32), proj, (subln_g[layer] * (1.0 - lambda_init))[None])

        w_s = jnp.where(sgu_mask[None], gmlp_w_s[layer], 0.0).astype(BF16)
        b_s = jnp.repeat(gmlp_b_s[layer].T, w_gmlp // gheads, axis=1)
        h, xn = _outproj(
            proj, y_tb, attn, h, ssm_d[layer][None], ssm_w_glu[layer].astype(BF16),
            ssm_b_glu[layer][None], gmlp_v_g[layer][None], w_s, b_s,
            w_out[layer].astype(BF16), ffn_norm_g[layer][None], seq=seq, w_ssm=w_ssm,
            w_gmlp=w_gmlp)
        h = _ffn(xn, h, w_up[layer].astype(BF16), conv_w[layer], conv_b[layer][None],
                 w_down[layer].astype(BF16), seq=seq)
    return h.reshape(batch, seq, d)
```

```python
import functools
import math

import jax
import jax.numpy as jnp
from jax import lax
from jax.experimental import pallas as pl
from jax.experimental.pallas import tpu as pltpu

BF16 = jnp.bfloat16
F32 = jnp.float32

EPS = 1e-6
ROPE_THETA = 10000.0
CHUNK = 64
LANES = 128
SUBLANES = 8
MASKED = -1e30
VMEM_LIMIT = 56 * 1024 * 1024
MAX_UNSHIFTED_SCORE = 60.0

TS_SSM = 256
TM_PROJ = 1024
SUB_PROJ = 512
TM_FFN = 1024
TF_FFN = 1408
GELU_C = math.sqrt(2.0 / math.pi)
GELU_K = 0.044715
TQ_ATTN = 512
TK_ATTN = 256
TQ_UNSHIFTED = 1024
TK_UNSHIFTED = 256


def _rms(x, g):
    return x * lax.rsqrt(jnp.mean(x * x, axis=-1, keepdims=True) + EPS) * g


def _params(*sem, flags=None):
    return pltpu.CompilerParams(dimension_semantics=sem, vmem_limit_bytes=VMEM_LIMIT, flags=flags)


def _inproj_kernel(h_ref, g_ref, w_ref, cos_ref, sin_ref, qg_ref, kg_ref, ones_ref,
                   o_ref, u_ref, *, o1, o3, o4, o5, slab, head_dim, sub):
    lane = lax.broadcasted_iota(jnp.int32, (sub, slab), 1)
    first_half = (lane & (head_dim - 1)) < head_dim // 2
    for r in range(0, h_ref.shape[0], sub):
        rows = slice(r, r + sub)
        xn = _rms(h_ref[rows, :], g_ref[...]).astype(BF16)
        front = jnp.dot(xn, w_ref[:, 0:o3], preferred_element_type=F32).astype(BF16)
        o_ref[rows, 0:o3] = front
        u_ref[rows, :] = front[:, 0:o1]
        o_ref[rows, o5:] = jnp.dot(xn, w_ref[:, o5:], preferred_element_type=F32).astype(BF16)
        cos = cos_ref[rows, :]
        sin = sin_ref[rows, :]
        for start, stop, gain_ref in ((o3, o4, qg_ref), (o4, o5, kg_ref)):
            for c in range(start, stop, slab):
                y = jnp.dot(xn, w_ref[:, c:c + slab], preferred_element_type=F32)
                y2 = (y * y).astype(BF16)
                piece = ones_ref.shape[0]
                ss = jnp.concatenate(
                    [jnp.dot(y2[:, q:q + piece], ones_ref[...], preferred_element_type=F32)
                     for q in range(0, slab, piece)], axis=1)
                y = y * lax.rsqrt(ss * (1.0 / head_dim) + EPS) * gain_ref[...]
                rot = jnp.where(first_half,
                                pltpu.roll(y, slab - head_dim // 2, 1),
                                pltpu.roll(y, head_dim // 2, 1))
                o_ref[rows, c:c + slab] = (y * cos + rot * sin).astype(BF16)


def _inproj(h, g, w, cos_t, sin_t, qg, kg, ones, *, layer, batch, seq, o1, o3, o4, o5, head_dim):
    n, d = h.shape
    cols = w.shape[2]
    tm = TM_PROJ
    slab = cos_t.shape[1]
    tps = seq // tm
    kern = functools.partial(_inproj_kernel, o1=o1, o3=o3, o4=o4, o5=o5, slab=slab,
                             head_dim=head_dim, sub=SUB_PROJ)
    return pl.pallas_call(
        kern,
        out_shape=(jax.ShapeDtypeStruct((n, cols), BF16),
                   jax.ShapeDtypeStruct((seq, batch * o1), BF16)),
        grid=(n // tm,),
        in_specs=[
            pl.BlockSpec((tm, d), lambda i: (i, 0)),
            pl.BlockSpec((1, d), lambda i: (0, 0)),
            pl.BlockSpec((None, d, cols), lambda i: (layer, 0, 0)),
            pl.BlockSpec((tm, slab), lambda i: (i % tps, 0)),
            pl.BlockSpec((tm, slab), lambda i: (i % tps, 0)),
            pl.BlockSpec((1, slab), lambda i: (0, 0)),
            pl.BlockSpec((1, slab), lambda i: (0, 0)),
            pl.BlockSpec(ones.shape, lambda i: (0, 0)),
        ],
        out_specs=(pl.BlockSpec((tm, cols), lambda i: (i, 0)),
                   pl.BlockSpec((tm, o1), lambda i: (i % tps, i // tps))),
        compiler_params=_params("parallel"),
        name="inproj",
    )(h, g, w, cos_t, sin_t, qg, kg, ones)


def _s5_kernel(u_ref, bt_ref, ct_ref, a_ref, y_ref, z_ref, x_ref, ub_ref, yb_ref,
               *, ts, bsz, ns):
    @pl.when(pl.program_id(0) == 0)
    def _():
        x_ref[...] = jnp.zeros(x_ref.shape, F32)

    width = u_ref.shape[1] // bsz
    slabs = width // LANES
    for b in range(bsz):
        ub = u_ref[:, b * width:(b + 1) * width].astype(F32)
        for s in range(slabs):
            ub_ref[s, pl.ds(b, ts, stride=bsz), :] = ub[:, s * LANES:(s + 1) * LANES]
    u = jnp.concatenate([ub_ref[s] for s in range(slabs)], axis=1).astype(BF16)

    half = ts * bsz // 2
    for r in (0, half):
        z_ref[r:r + half, :] = jnp.dot(u[r:r + half, :], bt_ref[...],
                                       preferred_element_type=F32)
    ar = jnp.broadcast_to(a_ref[0:1, :], (bsz, ns))
    ai = jnp.broadcast_to(a_ref[1:2, :], (bsz, ns))

    def body(t, carry):
        xr, xi = carry
        r = pl.multiple_of(t * bsz, bsz)
        nr = ar * xr - ai * xi + z_ref[pl.ds(r, bsz), 0:ns]
        ni = ar * xi + ai * xr + z_ref[pl.ds(r, bsz), ns:]
        z_ref[pl.ds(r, bsz), 0:ns] = nr
        z_ref[pl.ds(r, bsz), ns:] = ni
        return nr, ni

    xr, xi = lax.fori_loop(0, ts, body, (x_ref[:, 0:ns], x_ref[:, ns:]), unroll=4)
    x_ref[:, 0:ns] = xr
    x_ref[:, ns:] = xi
    for r in (0, half):
        yv = jnp.dot(z_ref[r:r + half, :].astype(BF16), ct_ref[...], preferred_element_type=F32)
        for s in range(slabs):
            yb_ref[s, r:r + half, :] = yv[:, s * LANES:(s + 1) * LANES]
    for b in range(bsz):
        for s in range(slabs):
            lanes = slice(b * width + s * LANES, b * width + (s + 1) * LANES)
            y_ref[:, lanes] = yb_ref[s, pl.ds(b, ts, stride=bsz), :].astype(BF16)


def _s5(u_tb, bt, ct, a, *, bsz):
    seq, cols = u_tb.shape
    width = cols // bsz
    ns = a.shape[1]
    ts = TS_SSM
    kern = functools.partial(_s5_kernel, ts=ts, bsz=bsz, ns=ns)
    slab = pltpu.VMEM((width // LANES, ts * bsz, LANES), F32)
    return pl.pallas_call(
        kern,
        out_shape=jax.ShapeDtypeStruct((seq, cols), BF16),
        grid=(seq // ts,),
        in_specs=[
            pl.BlockSpec((ts, cols), lambda i: (i, 0)),
            pl.BlockSpec((width, 2 * ns), lambda i: (0, 0)),
            pl.BlockSpec((2 * ns, width), lambda i: (0, 0)),
            pl.BlockSpec((2, ns), lambda i: (0, 0)),
        ],
        out_specs=pl.BlockSpec((ts, cols), lambda i: (i, 0)),
        scratch_shapes=[pltpu.VMEM((ts * bsz, 2 * ns), F32), pltpu.VMEM((bsz, 2 * ns), F32),
                        slab, slab],
        compiler_params=_params("arbitrary"),
        name="s5",
    )(u_tb, bt, ct, a)


def _s5_operators(a_re, a_im, log_dt, b_re, b_im, c_re, c_im):
    nl, ng, p = a_re.shape
    c = b_re.shape[-1]
    dt = jnp.exp(log_dt)[..., None]
    mag = jnp.exp(a_re * dt)
    ar = mag * jnp.cos(a_im * dt)
    ai = mag * jnp.sin(a_im * dt)
    den = a_re * a_re + a_im * a_im
    kr = ((ar - 1.0) * a_re + ai * a_im) / den
    ki = (ai * a_re - (ar - 1.0) * a_im) / den
    bbr = kr[..., None] * b_re - ki[..., None] * b_im
    bbi = kr[..., None] * b_im + ki[..., None] * b_re
    eye = jnp.eye(ng, dtype=F32)

    def expand_in(b):
        return jnp.einsum("lgpc,gh->lgchp", b, eye).reshape(nl, ng * c, ng * p)

    def expand_out(m):
        return jnp.einsum("lgcp,gh->lgphc", m, eye).reshape(nl, ng * p, ng * c)

    bt = jnp.concatenate([expand_in(bbr), expand_in(bbi)], axis=-1)
    ct = jnp.concatenate([expand_out(c_re), -expand_out(c_im)], axis=1)
    a = jnp.stack([ar.reshape(nl, ng * p), ai.reshape(nl, ng * p)], axis=1)
    return bt.astype(BF16), ct.astype(BF16), a


def _split_maps(q, half):
    lane = lax.broadcasted_iota(jnp.int32, q.shape, 1)
    zero = jnp.zeros_like(q)
    return jnp.where(lane < half, q, zero), jnp.where(lane >= half, q, zero)


def _chunk_mask(rows, cols, r0, c0):
    shift = CHUNK.bit_length() - 1
    row = lax.broadcasted_iota(jnp.int32, (rows, cols), 0) + r0
    col = lax.broadcasted_iota(jnp.int32, (rows, cols), 1) + c0
    return (col >> shift) <= (row >> shift)


def _attn_finish(acc0, l0, acc1, l1, lam, g_ref, o_ref):
    o = acc0 / l0 - lam * (acc1 / l1)
    o_ref[...] = _rms(o, g_ref[...]).astype(BF16)


def _attn_kernel(lam_ref, q_ref, k_ref, v_ref, g_ref, o_ref, m_ref, l_ref, acc_ref,
                 *, tq, tk, half):
    qi = pl.program_id(2)
    qs = _split_maps(q_ref[...], half)
    m_ref[...] = jnp.full(m_ref.shape, MASKED, F32)
    l_ref[...] = jnp.zeros(l_ref.shape, F32)
    acc_ref[...] = jnp.zeros(acc_ref.shape, F32)

    def block(r0, kstart, mask):
        kb = k_ref[pl.ds(kstart, tk), :]
        vb = v_ref[pl.ds(kstart, tk), :]
        for c in range(2):
            s = lax.dot_general(qs[c][r0:], kb, (((1,), (1,)), ((), ())),
                                preferred_element_type=F32)
            if mask is not None:
                s = jnp.where(mask, s, MASKED)
            m_prev = m_ref[c, r0:, :]
            m_new = jnp.maximum(m_prev, jnp.max(s, axis=1, keepdims=True))
            alpha = jnp.exp2(m_prev - m_new)
            p = jnp.exp2(s - jnp.concatenate([m_new] * (tk // LANES), axis=1))
            l_ref[c, r0:, :] = alpha * l_ref[c, r0:, :] + jnp.sum(p, axis=1, keepdims=True)
            acc_ref[c, r0:, :] = alpha * acc_ref[c, r0:, :] + jnp.dot(
                p.astype(BF16), vb, preferred_element_type=F32)
            m_ref[c, r0:, :] = m_new

    def body(j, carry):
        block(0, pl.multiple_of(j * tk, tk), None)
        return carry

    lax.fori_loop(0, qi * (tq // tk), body, 0)
    for jj in range(tq // tk):
        r0 = jj * tk
        block(r0, pl.multiple_of(qi * tq + jj * tk, tk), _chunk_mask(tq - r0, tk, r0, jj * tk))
    _attn_finish(acc_ref[0], l_ref[0], acc_ref[1], l_ref[1], lam_ref[0], g_ref, o_ref)


def _attn_unshifted_kernel(lam_ref, q_ref, k_ref, v_ref, g_ref, o_ref, l_ref, acc_ref,
                           *, tq, tk, half):
    qi = pl.program_id(2)
    qs = _split_maps(q_ref[...], half)

    def clear():
        l_ref[...] = jnp.zeros(l_ref.shape, F32)
        acc_ref[...] = jnp.zeros(acc_ref.shape, F32)

    pl.when((pl.program_id(0) == 0) & (pl.program_id(1) == 0) & (qi == 0))(clear)

    def block(r0, rows, kstart, ksize, mask):
        kb = k_ref[pl.ds(kstart, ksize), :]
        vb = v_ref[pl.ds(kstart, ksize), :]
        for c in range(2):
            s = lax.dot_general(qs[c][r0:r0 + rows], kb, (((1,), (1,)), ((), ())),
                                preferred_element_type=F32)
            p = jnp.exp2(s)
            if mask is not None:
                p = jnp.where(mask, p, 0.0)
            part = p[:, 0:LANES]
            for k in range(1, ksize // LANES):
                part = part + p[:, k * LANES:(k + 1) * LANES]
            l_ref[c, r0:r0 + rows, :] += part
            acc_ref[c, r0:r0 + rows, :] += jnp.dot(p.astype(BF16), vb,
                                                   preferred_element_type=F32)

    def body(j, carry):
        block(0, tq, pl.multiple_of(j * tq, tq), tq, None)
        return carry

    lax.fori_loop(0, qi, body, 0)
    for jj in range(tq // tk):
        r0 = jj * tk
        block(r0, tq - r0, pl.multiple_of(qi * tq + jj * tk, tk), tk,
              _chunk_mask(tq - r0, tk, r0, jj * tk))
    l0 = jnp.sum(l_ref[0], axis=1, keepdims=True)
    l1 = jnp.sum(l_ref[1], axis=1, keepdims=True)
    _attn_finish(acc_ref[0], l0, acc_ref[1], l1, lam_ref[0], g_ref, o_ref)
    clear()


def _attention(lam, proj, sub_g, *, unshifted, batch, seq, heads, vdim, qcol, kcol, vcol):
    tq, tk = (TQ_UNSHIFTED, TK_UNSHIFTED) if unshifted else (TQ_ATTN, TK_ATTN)
    nq = seq // tq
    acc = pltpu.VMEM((2, tq, vdim), F32)
    stat = pltpu.VMEM((2, tq, LANES), F32)
    if unshifted:
        kern = functools.partial(_attn_unshifted_kernel, tq=tq, tk=tk, half=vdim // 2)
        scratch = [stat, acc]
        name = "diffattn_unshifted"
    else:
        kern = functools.partial(_attn_kernel, tq=tq, tk=tk, half=vdim // 2)
        scratch = [stat, stat, acc]
        name = "diffattn"
    return pl.pallas_call(
        kern,
        out_shape=jax.ShapeDtypeStruct((batch * seq, heads * vdim), BF16),
        grid=(batch, heads, nq),
        in_specs=[
            pl.BlockSpec(memory_space=pltpu.SMEM),
            pl.BlockSpec((tq, vdim), lambda b, h, i: (b * nq + i, qcol + h)),
            pl.BlockSpec((seq, vdim), lambda b, h, i: (b, kcol + h)),
            pl.BlockSpec((seq, vdim), lambda b, h, i: (b, vcol + h)),
            pl.BlockSpec((1, vdim), lambda b, h, i: (0, 0)),
        ],
        out_specs=pl.BlockSpec((tq, vdim), lambda b, h, i: (b * nq + i, h)),
        scratch_shapes=scratch,
        compiler_params=_params("arbitrary", "arbitrary", "arbitrary"),
        name=name,
    )(lam, proj, proj, proj, sub_g)


def _outproj_kernel(p_ref, y_ref, a_ref, h_ref, d_ref, wglu_ref, bglu_ref, vgain_ref, ws_ref,
                    bs_ref, wo_ref, g2_ref, ho_ref, xn_ref, sg_ref,
                    *, w_ssm, w_gmlp, win, heads, sub):
    head_shift = (w_gmlp // heads).bit_length() - 1
    head_of_lane = lax.broadcasted_iota(jnp.int32, (win, w_gmlp), 1) >> head_shift
    for r in range(0, h_ref.shape[0], sub):
        rows = slice(r, r + sub)
        p = p_ref[rows, :]
        u = p[:, 0:w_ssm].astype(F32)
        ug = p[:, w_ssm:w_ssm + w_gmlp].astype(F32)
        vg = p[:, w_ssm + w_gmlp:].astype(F32)

        y = jax.nn.gelu(y_ref[rows, :].astype(F32) + d_ref[...] * u)
        z = jnp.dot(y.astype(BF16), wglu_ref[...], preferred_element_type=F32) + bglu_ref[...]
        y_ssm = (y * jax.nn.sigmoid(z)).astype(BF16)

        v = _rms(vg, vgain_ref[...])
        for w in range(sub // win):
            vw = v[w * win:(w + 1) * win, :]
            mixed = bs_ref[...]
            for hh in range(heads):
                vh = jnp.where(head_of_lane == hh, vw, 0.0).astype(BF16)
                mixed = mixed + jnp.dot(ws_ref[hh], vh, preferred_element_type=F32)
            sg_ref[r + w * win:r + (w + 1) * win, :] = (
                ug[w * win:(w + 1) * win, :] * mixed).astype(BF16)

        out = jnp.dot(y_ssm, wo_ref[0:w_ssm, :], preferred_element_type=F32)
        out = out + jnp.dot(sg_ref[rows, :], wo_ref[w_ssm:w_ssm + w_gmlp, :],
                            preferred_element_type=F32)
        out = out + jnp.dot(a_ref[rows, :], wo_ref[w_ssm + w_gmlp:, :],
                            preferred_element_type=F32)
        h_new = h_ref[rows, :] + out
        ho_ref[rows, :] = h_new
        xn_ref[rows, :] = _rms(h_new, g2_ref[...]).astype(BF16)


def _outproj(proj, y_tb, attn, h, d_skip, w_glu, b_glu, v_gain, w_s, b_s, w_out, g2,
             *, layer, seq, w_ssm, w_gmlp):
    n, d = h.shape
    tm = TM_PROJ
    tps = seq // tm
    heads, win, _ = w_s.shape
    w_diff = attn.shape[1]
    front = w_ssm + 2 * w_gmlp
    assert (w_gmlp // heads) & (w_gmlp // heads - 1) == 0
    kern = functools.partial(_outproj_kernel, w_ssm=w_ssm, w_gmlp=w_gmlp, win=win, heads=heads,
                             sub=SUB_PROJ)
    const2 = lambda i: (0, 0)
    return pl.pallas_call(
        kern,
        out_shape=(jax.ShapeDtypeStruct((n, d), F32), jax.ShapeDtypeStruct((n, d), BF16)),
        grid=(n // tm,),
        in_specs=[
            pl.BlockSpec((tm, front), lambda i: (i, 0)),
            pl.BlockSpec((tm, w_ssm), lambda i: (i % tps, i // tps)),
            pl.BlockSpec((tm, w_diff), lambda i: (i, 0)),
            pl.BlockSpec((tm, d), lambda i: (i, 0)),
            pl.BlockSpec((1, w_ssm), const2),
            pl.BlockSpec((w_ssm, w_ssm), const2),
            pl.BlockSpec((1, w_ssm), const2),
            pl.BlockSpec((1, w_gmlp), const2),
            pl.BlockSpec((heads, win, win), lambda i: (0, 0, 0)),
            pl.BlockSpec((win, w_gmlp), const2),
            pl.BlockSpec((None, w_ssm + w_gmlp + w_diff, d), lambda i: (layer, 0, 0)),
            pl.BlockSpec((1, d), const2),
        ],
        out_specs=(pl.BlockSpec((tm, d), lambda i: (i, 0)), pl.BlockSpec((tm, d), lambda i: (i, 0))),
        scratch_shapes=[pltpu.VMEM((tm, w_gmlp), BF16)],
        compiler_params=_params("parallel"),
        name="outproj",
    )(proj, y_tb, attn, h, d_skip, w_glu, b_glu, v_gain, w_s, b_s, w_out, g2)


def _ffn_kernel(x_ref, halo_ref, h_ref, wg_ref, wv_ref, cwg_ref, cwv_ref, cbg_ref, cbv_ref,
                wd_ref, o_ref, sg_ref, sv_ref, *, tm, tiles_per_seq, taps):
    i = pl.program_id(0)

    @pl.when(pl.program_id(1) == 0)
    def _():
        o_ref[...] = h_ref[...]

    halo = halo_ref[...]
    halo = jnp.where(i % tiles_per_seq == 0, jnp.zeros_like(halo), halo)
    x = x_ref[...]

    def conv(w_ref, s_ref, cw_ref, cb_ref):
        s_ref[0:SUBLANES, :] = jnp.dot(halo, w_ref[...], preferred_element_type=F32)
        s_ref[SUBLANES:, :] = jnp.dot(x, w_ref[...], preferred_element_type=F32)
        full = s_ref[...]
        acc = cb_ref[...] + cw_ref[taps - 1:taps, :] * full[SUBLANES:, :]
        for back in range(1, taps):
            shifted = pltpu.roll(full, back, 0)[SUBLANES:, :]
            acc = acc + cw_ref[taps - 1 - back:taps - back, :] * shifted
        return acc

    gate = conv(wg_ref, sg_ref, cwg_ref, cbg_ref)
    half_val = conv(wv_ref, sv_ref, cwv_ref, cbv_ref)
    inner = gate * (GELU_C + (GELU_C * GELU_K) * (gate * gate))
    a = (gate * (1.0 + jnp.tanh(inner)) * half_val).astype(BF16)
    o_ref[...] += jnp.dot(a, wd_ref[...], preferred_element_type=F32)


def _ffn(xn, h, w_up, conv_w, conv_b, w_down, *, layer, seq):
    n, d = h.shape
    f = w_down.shape[1]
    taps = conv_w.shape[0]
    tm, tf = TM_FFN, TF_FFN
    nf = f // tf
    tiles_per_seq = seq // tm
    kern = functools.partial(_ffn_kernel, tm=tm, tiles_per_seq=tiles_per_seq, taps=taps)
    halo_blocks = tm // SUBLANES
    weight_mode = pl.Buffered(1) if nf == 1 else None
    half = jnp.concatenate([jnp.ones((f,), F32), jnp.full((f,), 0.5, F32)])
    conv_w = conv_w * half
    conv_b = conv_b * half
    return pl.pallas_call(
        kern,
        out_shape=jax.ShapeDtypeStruct((n, d), F32),
        grid=(n // tm, nf),
        in_specs=[
            pl.BlockSpec((tm, d), lambda i, j: (i, 0)),
            pl.BlockSpec((SUBLANES, d), lambda i, j: (jnp.maximum(i * halo_blocks - 1, 0), 0)),
            pl.BlockSpec((tm, d), lambda i, j: (i, 0)),
            pl.BlockSpec((None, d, tf), lambda i, j: (layer, 0, j), pipeline_mode=weight_mode),
            pl.BlockSpec((None, d, tf), lambda i, j: (layer, 0, nf + j),
                         pipeline_mode=weight_mode),
            pl.BlockSpec((taps, tf), lambda i, j: (0, j)),
            pl.BlockSpec((taps, tf), lambda i, j: (0, nf + j)),
            pl.BlockSpec((1, tf), lambda i, j: (0, j)),
            pl.BlockSpec((1, tf), lambda i, j: (0, nf + j)),
            pl.BlockSpec((None, tf, d), lambda i, j: (layer, j, 0), pipeline_mode=weight_mode),
        ],
        out_specs=pl.BlockSpec((tm, d), lambda i, j: (i, 0)),
        scratch_shapes=[pltpu.VMEM((tm + SUBLANES, tf), F32), pltpu.VMEM((tm + SUBLANES, tf), F32)],
        compiler_params=_params("parallel", "arbitrary"),
        name="convffn",
    )(xn, xn, h, w_up, w_up, conv_w, conv_w, conv_b, conv_b, w_down)


def _rope_tables(seq, head_dim, reps):
    half = head_dim // 2
    inv = ROPE_THETA ** (-jnp.arange(half, dtype=F32) / half)
    ang = jnp.arange(seq, dtype=F32)[:, None] * inv[None, :]
    ang = jnp.concatenate([ang, ang], axis=-1)
    sign = jnp.concatenate([-jnp.ones((half,), F32), jnp.ones((half,), F32)])
    return jnp.tile(jnp.cos(ang), (1, reps)), jnp.tile(jnp.sin(ang) * sign, (1, reps))


def kernel(x, attn_norm_g, w_in, ssm_a_re, ssm_a_im, ssm_log_dt, ssm_b_re, ssm_b_im, ssm_c_re, ssm_c_im, ssm_d, ssm_w_glu, ssm_b_glu, gmlp_v_g, gmlp_w_s, gmlp_b_s, q_norm_g, k_norm_g, lambda_q1, lambda_k1, lambda_q2, lambda_k2, subln_g, w_out, ffn_norm_g, w_up, conv_w, conv_b, w_down):
    batch, seq, d = x.shape
    depth, _, in_cols = w_in.shape
    w_ssm = ssm_d.shape[1]
    w_gmlp = gmlp_v_g.shape[1]
    w_diff = (in_cols - w_ssm - 2 * w_gmlp) // 3
    qk_dim = q_norm_g.shape[1]
    vdim = subln_g.shape[1]
    heads = w_diff // vdim
    gheads, win = gmlp_w_s.shape[1], gmlp_w_s.shape[2]
    o3 = w_ssm + 2 * w_gmlp
    o4 = o3 + w_diff
    o5 = o4 + w_diff
    assert vdim == LANES and 2 * qk_dim == vdim and batch == SUBLANES

    slab = w_diff
    cos_t, sin_t = _rope_tables(seq, qk_dim, slab // qk_dim)
    lane = jnp.arange(2 * LANES)
    ones = (lane[:, None] // qk_dim == lane[None, :] // qk_dim).astype(BF16)
    q_scale = qk_dim ** -0.5 * math.log2(math.e)
    bt, ct, a_bar = _s5_operators(ssm_a_re, ssm_a_im, ssm_log_dt, ssm_b_re, ssm_b_im,
                                  ssm_c_re, ssm_c_im)
    pos_chunk = jnp.arange(win) // CHUNK
    sgu_mask = pos_chunk[None, :] <= pos_chunk[:, None]
    attn_args = dict(batch=batch, seq=seq, heads=heads, vdim=vdim,
                     qcol=o3 // vdim, kcol=o4 // vdim, vcol=o5 // vdim)

    w_in_bf, w_out_bf = w_in.astype(BF16), w_out.astype(BF16)
    w_up_bf, w_down_bf = w_up.astype(BF16), w_down.astype(BF16)

    h = x.reshape(batch * seq, d)
    for layer in range(depth):
        lambda_init = 0.8 - 0.6 * math.exp(-0.3 * layer)
        lam = (jnp.exp(jnp.sum(lambda_q1[layer] * lambda_k1[layer]))
               - jnp.exp(jnp.sum(lambda_q2[layer] * lambda_k2[layer])) + lambda_init)
        q_gain = q_norm_g[layer] * q_scale
        proj, u_tb = _inproj(
            h, attn_norm_g[layer][None], w_in_bf, cos_t, sin_t,
            jnp.tile(q_gain, slab // qk_dim)[None],
            jnp.tile(k_norm_g[layer], slab // qk_dim)[None], ones,
            layer=layer, batch=batch, seq=seq, o1=w_ssm, o3=o3, o4=o4, o5=o5, head_dim=qk_dim)

        y_tb = _s5(u_tb, bt[layer], ct[layer], a_bar[layer], bsz=batch)

        score_bound = qk_dim * jnp.max(jnp.abs(q_gain)) * jnp.max(jnp.abs(k_norm_g[layer]))
        attn = lax.cond(
            score_bound <= MAX_UNSHIFTED_SCORE,
            functools.partial(_attention, unshifted=True, **attn_args),
            functools.partial(_attention, unshifted=False, **attn_args),
            lam.reshape(1).astype(F32), proj, (subln_g[layer] * (1.0 - lambda_init))[None])

        w_s = jnp.where(sgu_mask[None], gmlp_w_s[layer], 0.0).astype(BF16)
        b_s = jnp.repeat(gmlp_b_s[layer].T, w_gmlp // gheads, axis=1)
        h, xn = _outproj(
            proj, y_tb, attn, h, ssm_d[layer][None], ssm_w_glu[layer].astype(BF16),
            ssm_b_glu[layer][None], gmlp_v_g[layer][None], w_s, b_s,
            w_out_bf, ffn_norm_g[layer][None], layer=layer, seq=seq, w_ssm=w_ssm,
            w_gmlp=w_gmlp)
        h = _ffn(xn, h, w_up_bf, conv_w[layer], conv_b[layer][None], w_down_bf,
                 layer=layer, seq=seq)
    return h.reshape(batch, seq, d)
```

```python
import functools
import math

import jax
import jax.numpy as jnp
from jax import lax
from jax.experimental import pallas as pl
from jax.experimental.pallas import tpu as pltpu

BF16 = jnp.bfloat16
F32 = jnp.float32

EPS = 1e-6
ROPE_THETA = 10000.0
CHUNK = 64
LANES = 128
SUBLANES = 8
MASKED = -1e30
VMEM_LIMIT = 56 * 1024 * 1024
MAX_UNSHIFTED_SCORE = 60.0

TS_SSM = 256
TM_PROJ = 1024
SUB_PROJ = 512
TM_FFN = 1024
TF_FFN = 1408
GELU_C = math.sqrt(2.0 / math.pi)
GELU_K = 0.044715
TQ_ATTN = 512
TK_ATTN = 256
TQ_UNSHIFTED = 1024
TK_UNSHIFTED = 256


def _rms(x, g):
    return x * lax.rsqrt(jnp.mean(x * x, axis=-1, keepdims=True) + EPS) * g


def _params(*sem):
    return pltpu.CompilerParams(dimension_semantics=sem, vmem_limit_bytes=VMEM_LIMIT)


def _inproj_kernel(h_ref, g_ref, w_ref, cos_ref, sin_ref, qg_ref, kg_ref, ones_ref,
                   o_ref, u_ref, *, o1, o3, o4, o5, slab, head_dim, sub):
    lane = lax.broadcasted_iota(jnp.int32, (sub, slab), 1)
    first_half = (lane & (head_dim - 1)) < head_dim // 2
    for r in range(0, h_ref.shape[0], sub):
        rows = slice(r, r + sub)
        xn = _rms(h_ref[rows, :], g_ref[...]).astype(BF16)
        front = jnp.dot(xn, w_ref[:, 0:o3], preferred_element_type=F32).astype(BF16)
        o_ref[rows, 0:o3] = front
        u_ref[rows, :] = front[:, 0:o1]
        o_ref[rows, o5:] = jnp.dot(xn, w_ref[:, o5:], preferred_element_type=F32).astype(BF16)
        cos = cos_ref[rows, :]
        sin = sin_ref[rows, :]
        for start, stop, gain_ref in ((o3, o4, qg_ref), (o4, o5, kg_ref)):
            for c in range(start, stop, slab):
                y = jnp.dot(xn, w_ref[:, c:c + slab], preferred_element_type=F32)
                y2 = (y * y).astype(BF16)
                piece = ones_ref.shape[0]
                ss = jnp.concatenate(
                    [jnp.dot(y2[:, q:q + piece], ones_ref[...], preferred_element_type=F32)
                     for q in range(0, slab, piece)], axis=1)
                y = y * lax.rsqrt(ss * (1.0 / head_dim) + EPS) * gain_ref[...]
                rot = jnp.where(first_half,
                                pltpu.roll(y, slab - head_dim // 2, 1),
                                pltpu.roll(y, head_dim // 2, 1))
                o_ref[rows, c:c + slab] = (y * cos + rot * sin).astype(BF16)


def _inproj(h, g, w, cos_t, sin_t, qg, kg, ones, *, layer, batch, seq, o1, o3, o4, o5, head_dim):
    n, d = h.shape
    cols = w.shape[2]
    tm = TM_PROJ
    slab = cos_t.shape[1]
    tps = seq // tm
    kern = functools.partial(_inproj_kernel, o1=o1, o3=o3, o4=o4, o5=o5, slab=slab,
                             head_dim=head_dim, sub=SUB_PROJ)
    return pl.pallas_call(
        kern,
        out_shape=(jax.ShapeDtypeStruct((n, cols), BF16),
                   jax.ShapeDtypeStruct((seq, batch * o1), BF16)),
        grid=(n // tm,),
        in_specs=[
            pl.BlockSpec((tm, d), lambda i: (i, 0)),
            pl.BlockSpec((1, d), lambda i: (0, 0)),
            pl.BlockSpec((None, d, cols), lambda i: (layer, 0, 0)),
            pl.BlockSpec((tm, slab), lambda i: (i % tps, 0)),
            pl.BlockSpec((tm, slab), lambda i: (i % tps, 0)),
            pl.BlockSpec((1, slab), lambda i: (0, 0)),
            pl.BlockSpec((1, slab), lambda i: (0, 0)),
            pl.BlockSpec(ones.shape, lambda i: (0, 0)),
        ],
        out_specs=(pl.BlockSpec((tm, cols), lambda i: (i, 0)),
                   pl.BlockSpec((tm, o1), lambda i: (i % tps, i // tps))),
        compiler_params=_params("parallel"),
        name="inproj",
    )(h, g, w, cos_t, sin_t, qg, kg, ones)


def _s5_kernel(u_ref, bt_ref, ct_ref, a_ref, y_ref, z_ref, x_ref, ub_ref, yb_ref,
               *, ts, bsz, ns):
    @pl.when(pl.program_id(0) == 0)
    def _():
        x_ref[...] = jnp.zeros(x_ref.shape, F32)

    width = u_ref.shape[1] // bsz
    slabs = width // LANES
    for b in range(bsz):
        ub = u_ref[:, b * width:(b + 1) * width].astype(F32)
        for s in range(slabs):
            ub_ref[s, pl.ds(b, ts, stride=bsz), :] = ub[:, s * LANES:(s + 1) * LANES]
    u = jnp.concatenate([ub_ref[s] for s in range(slabs)], axis=1).astype(BF16)

    half = ts * bsz // 2
    for r in (0, half):
        z_ref[r:r + half, :] = jnp.dot(u[r:r + half, :], bt_ref[...],
                                       preferred_element_type=F32)
    ar = jnp.broadcast_to(a_ref[0:1, :], (bsz, ns))
    ai = jnp.broadcast_to(a_ref[1:2, :], (bsz, ns))

    def body(t, carry):
        xr, xi = carry
        r = pl.multiple_of(t * bsz, bsz)
        nr = ar * xr - ai * xi + z_ref[pl.ds(r, bsz), 0:ns]
        ni = ar * xi + ai * xr + z_ref[pl.ds(r, bsz), ns:]
        z_ref[pl.ds(r, bsz), 0:ns] = nr
        z_ref[pl.ds(r, bsz), ns:] = ni
        return nr, ni

    xr, xi = lax.fori_loop(0, ts, body, (x_ref[:, 0:ns], x_ref[:, ns:]), unroll=8)
    x_ref[:, 0:ns] = xr
    x_ref[:, ns:] = xi
    for r in (0, half):
        yv = jnp.dot(z_ref[r:r + half, :].astype(BF16), ct_ref[...], preferred_element_type=F32)
        for s in range(slabs):
            yb_ref[s, r:r + half, :] = yv[:, s * LANES:(s + 1) * LANES]
    for b in range(bsz):
        for s in range(slabs):
            lanes = slice(b * width + s * LANES, b * width + (s + 1) * LANES)
            y_ref[:, lanes] = yb_ref[s, pl.ds(b, ts, stride=bsz), :].astype(BF16)


def _s5(u_tb, bt, ct, a, *, bsz):
    seq, cols = u_tb.shape
    width = cols // bsz
    ns = a.shape[1]
    ts = TS_SSM
    kern = functools.partial(_s5_kernel, ts=ts, bsz=bsz, ns=ns)
    slab = pltpu.VMEM((width // LANES, ts * bsz, LANES), F32)
    return pl.pallas_call(
        kern,
        out_shape=jax.ShapeDtypeStruct((seq, cols), BF16),
        grid=(seq // ts,),
        in_specs=[
            pl.BlockSpec((ts, cols), lambda i: (i, 0)),
            pl.BlockSpec((width, 2 * ns), lambda i: (0, 0)),
            pl.BlockSpec((2 * ns, width), lambda i: (0, 0)),
            pl.BlockSpec((2, ns), lambda i: (0, 0)),
        ],
        out_specs=pl.BlockSpec((ts, cols), lambda i: (i, 0)),
        scratch_shapes=[pltpu.VMEM((ts * bsz, 2 * ns), F32), pltpu.VMEM((bsz, 2 * ns), F32),
                        slab, slab],
        compiler_params=_params("arbitrary"),
        name="s5",
    )(u_tb, bt, ct, a)


def _s5_operators(a_re, a_im, log_dt, b_re, b_im, c_re, c_im):
    nl, ng, p = a_re.shape
    c = b_re.shape[-1]
    dt = jnp.exp(log_dt)[..., None]
    mag = jnp.exp(a_re * dt)
    ar = mag * jnp.cos(a_im * dt)
    ai = mag * jnp.sin(a_im * dt)
    den = a_re * a_re + a_im * a_im
    kr = ((ar - 1.0) * a_re + ai * a_im) / den
    ki = (ai * a_re - (ar - 1.0) * a_im) / den
    bbr = kr[..., None] * b_re - ki[..., None] * b_im
    bbi = kr[..., None] * b_im + ki[..., None] * b_re
    eye = jnp.eye(ng, dtype=F32)

    def expand_in(b):
        return jnp.einsum("lgpc,gh->lgchp", b, eye).reshape(nl, ng * c, ng * p)

    def expand_out(m):
        return jnp.einsum("lgcp,gh->lgphc", m, eye).reshape(nl, ng * p, ng * c)

    bt = jnp.concatenate([expand_in(bbr), expand_in(bbi)], axis=-1)
    ct = jnp.concatenate([expand_out(c_re), -expand_out(c_im)], axis=1)
    a = jnp.stack([ar.reshape(nl, ng * p), ai.reshape(nl, ng * p)], axis=1)
    return bt.astype(BF16), ct.astype(BF16), a


def _split_maps(q, half):
    lane = lax.broadcasted_iota(jnp.int32, q.shape, 1)
    zero = jnp.zeros_like(q)
    return jnp.where(lane < half, q, zero), jnp.where(lane >= half, q, zero)


def _chunk_mask(rows, cols, r0, c0):
    shift = CHUNK.bit_length() - 1
    row = lax.broadcasted_iota(jnp.int32, (rows, cols), 0) + r0
    col = lax.broadcasted_iota(jnp.int32, (rows, cols), 1) + c0
    return (col >> shift) <= (row >> shift)


def _attn_finish(acc0, l0, acc1, l1, lam, g_ref, o_ref):
    o = acc0 / l0 - lam * (acc1 / l1)
    o_ref[...] = _rms(o, g_ref[...]).astype(BF16)


def _attn_kernel(lam_ref, q_ref, k_ref, v_ref, g_ref, o_ref, m_ref, l_ref, acc_ref,
                 *, tq, tk, half):
    qi = pl.program_id(2)
    qs = _split_maps(q_ref[...], half)
    m_ref[...] = jnp.full(m_ref.shape, MASKED, F32)
    l_ref[...] = jnp.zeros(l_ref.shape, F32)
    acc_ref[...] = jnp.zeros(acc_ref.shape, F32)

    def block(r0, kstart, mask):
        kb = k_ref[pl.ds(kstart, tk), :]
        vb = v_ref[pl.ds(kstart, tk), :]
        for c in range(2):
            s = lax.dot_general(qs[c][r0:], kb, (((1,), (1,)), ((), ())),
                                preferred_element_type=F32)
            if mask is not None:
                s = jnp.where(mask, s, MASKED)
            m_prev = m_ref[c, r0:, :]
            m_new = jnp.maximum(m_prev, jnp.max(s, axis=1, keepdims=True))
            alpha = jnp.exp2(m_prev - m_new)
            p = jnp.exp2(s - jnp.concatenate([m_new] * (tk // LANES), axis=1))
            l_ref[c, r0:, :] = alpha * l_ref[c, r0:, :] + jnp.sum(p, axis=1, keepdims=True)
            acc_ref[c, r0:, :] = alpha * acc_ref[c, r0:, :] + jnp.dot(
                p.astype(BF16), vb, preferred_element_type=F32)
            m_ref[c, r0:, :] = m_new

    def body(j, carry):
        block(0, pl.multiple_of(j * tk, tk), None)
        return carry

    lax.fori_loop(0, qi * (tq // tk), body, 0)
    for jj in range(tq // tk):
        r0 = jj * tk
        block(r0, pl.multiple_of(qi * tq + jj * tk, tk), _chunk_mask(tq - r0, tk, r0, jj * tk))
    _attn_finish(acc_ref[0], l_ref[0], acc_ref[1], l_ref[1], lam_ref[0], g_ref, o_ref)


def _attn_unshifted_kernel(lam_ref, q_ref, k_ref, v_ref, g_ref, o_ref, l_ref, acc_ref,
                           *, tq, tk, half):
    qi = pl.program_id(2)
    qs = _split_maps(q_ref[...], half)

    def clear():
        l_ref[...] = jnp.zeros(l_ref.shape, F32)
        acc_ref[...] = jnp.zeros(acc_ref.shape, F32)

    pl.when((pl.program_id(0) == 0) & (pl.program_id(1) == 0) & (qi == 0))(clear)

    def block(r0, rows, kstart, ksize, mask):
        kb = k_ref[pl.ds(kstart, ksize), :]
        vb = v_ref[pl.ds(kstart, ksize), :]
        for c in range(2):
            s = lax.dot_general(qs[c][r0:r0 + rows], kb, (((1,), (1,)), ((), ())),
                                preferred_element_type=F32)
            p = jnp.exp2(s)
            if mask is not None:
                p = jnp.where(mask, p, 0.0)
            part = p[:, 0:LANES]
            for k in range(1, ksize // LANES):
                part = part + p[:, k * LANES:(k + 1) * LANES]
            l_ref[c, r0:r0 + rows, :] += part
            acc_ref[c, r0:r0 + rows, :] += jnp.dot(p.astype(BF16), vb,
                                                   preferred_element_type=F32)

    def body(j, carry):
        block(0, tq, pl.multiple_of(j * tq, tq), tq, None)
        return carry

    lax.fori_loop(0, qi, body, 0)
    for jj in range(tq // tk):
        r0 = jj * tk
        block(r0, tq - r0, pl.multiple_of(qi * tq + jj * tk, tk), tk,
              _chunk_mask(tq - r0, tk, r0, jj * tk))
    l0 = jnp.sum(l_ref[0], axis=1, keepdims=True)
    l1 = jnp.sum(l_ref[1], axis=1, keepdims=True)
    _attn_finish(acc_ref[0], l0, acc_ref[1], l1, lam_ref[0], g_ref, o_ref)
    clear()


def _attention(lam, proj, sub_g, *, unshifted, batch, seq, heads, vdim, qcol, kcol, vcol):
    tq, tk = (TQ_UNSHIFTED, TK_UNSHIFTED) if unshifted else (TQ_ATTN, TK_ATTN)
    nq = seq // tq
    acc = pltpu.VMEM((2, tq, vdim), F32)
    stat = pltpu.VMEM((2, tq, LANES), F32)
    if unshifted:
        kern = functools.partial(_attn_unshifted_kernel, tq=tq, tk=tk, half=vdim // 2)
        scratch = [stat, acc]
        name = "diffattn_unshifted"
    else:
        kern = functools.partial(_attn_kernel, tq=tq, tk=tk, half=vdim // 2)
        scratch = [stat, stat, acc]
        name = "diffattn"
    return pl.pallas_call(
        kern,
        out_shape=jax.ShapeDtypeStruct((batch * seq, heads * vdim), BF16),
        grid=(batch, heads, nq),
        in_specs=[
            pl.BlockSpec(memory_space=pltpu.SMEM),
            pl.BlockSpec((tq, vdim), lambda b, h, i: (b * nq + i, qcol + h)),
            pl.BlockSpec((seq, vdim), lambda b, h, i: (b, kcol + h)),
            pl.BlockSpec((seq, vdim), lambda b, h, i: (b, vcol + h)),
            pl.BlockSpec((1, vdim), lambda b, h, i: (0, 0)),
        ],
        out_specs=pl.BlockSpec((tq, vdim), lambda b, h, i: (b * nq + i, h)),
        scratch_shapes=scratch,
        compiler_params=_params("arbitrary", "arbitrary", "arbitrary"),
        name=name,
    )(lam, proj, proj, proj, sub_g)


def _outproj_kernel(p_ref, y_ref, a_ref, h_ref, d_ref, wglu_ref, bglu_ref, vgain_ref, ws_ref,
                    bs_ref, wo_ref, g2_ref, ho_ref, xn_ref, sg_ref,
                    *, w_ssm, w_gmlp, win, heads, sub):
    head_shift = (w_gmlp // heads).bit_length() - 1
    head_of_lane = lax.broadcasted_iota(jnp.int32, (win, w_gmlp), 1) >> head_shift
    for r in range(0, h_ref.shape[0], sub):
        rows = slice(r, r + sub)
        p = p_ref[rows, :]
        u = p[:, 0:w_ssm].astype(F32)
        ug = p[:, w_ssm:w_ssm + w_gmlp].astype(F32)
        vg = p[:, w_ssm + w_gmlp:].astype(F32)

        y = jax.nn.gelu(y_ref[rows, :].astype(F32) + d_ref[...] * u)
        z = jnp.dot(y.astype(BF16), wglu_ref[...], preferred_element_type=F32) + bglu_ref[...]
        y_ssm = (y * jax.nn.sigmoid(z)).astype(BF16)

        v = _rms(vg, vgain_ref[...])
        for w in range(sub // win):
            vw = v[w * win:(w + 1) * win, :]
            mixed = bs_ref[...]
            for hh in range(heads):
                vh = jnp.where(head_of_lane == hh, vw, 0.0).astype(BF16)
                mixed = mixed + jnp.dot(ws_ref[hh], vh, preferred_element_type=F32)
            sg_ref[r + w * win:r + (w + 1) * win, :] = (
                ug[w * win:(w + 1) * win, :] * mixed).astype(BF16)

        out = jnp.dot(y_ssm, wo_ref[0:w_ssm, :], preferred_element_type=F32)
        out = out + jnp.dot(sg_ref[rows, :], wo_ref[w_ssm:w_ssm + w_gmlp, :],
                            preferred_element_type=F32)
        out = out + jnp.dot(a_ref[rows, :], wo_ref[w_ssm + w_gmlp:, :],
                            preferred_element_type=F32)
        h_new = h_ref[rows, :] + out
        ho_ref[rows, :] = h_new
        xn_ref[rows, :] = _rms(h_new, g2_ref[...]).astype(BF16)


def _outproj(proj, y_tb, attn, h, d_skip, w_glu, b_glu, v_gain, w_s, b_s, w_out, g2,
             *, layer, seq, w_ssm, w_gmlp):
    n, d = h.shape
    tm = TM_PROJ
    tps = seq // tm
    heads, win, _ = w_s.shape
    w_diff = attn.shape[1]
    front = w_ssm + 2 * w_gmlp
    assert (w_gmlp // heads) & (w_gmlp // heads - 1) == 0
    kern = functools.partial(_outproj_kernel, w_ssm=w_ssm, w_gmlp=w_gmlp, win=win, heads=heads,
                             sub=SUB_PROJ)
    const2 = lambda i: (0, 0)
    return pl.pallas_call(
        kern,
        out_shape=(jax.ShapeDtypeStruct((n, d), F32), jax.ShapeDtypeStruct((n, d), BF16)),
        grid=(n // tm,),
        in_specs=[
            pl.BlockSpec((tm, front), lambda i: (i, 0)),
            pl.BlockSpec((tm, w_ssm), lambda i: (i % tps, i // tps)),
            pl.BlockSpec((tm, w_diff), lambda i: (i, 0)),
            pl.BlockSpec((tm, d), lambda i: (i, 0)),
            pl.BlockSpec((1, w_ssm), const2),
            pl.BlockSpec((w_ssm, w_ssm), const2),
            pl.BlockSpec((1, w_ssm), const2),
            pl.BlockSpec((1, w_gmlp), const2),
            pl.BlockSpec((heads, win, win), lambda i: (0, 0, 0)),
            pl.BlockSpec((win, w_gmlp), const2),
            pl.BlockSpec((None, w_ssm + w_gmlp + w_diff, d), lambda i: (layer, 0, 0)),
            pl.BlockSpec((1, d), const2),
        ],
        out_specs=(pl.BlockSpec((tm, d), lambda i: (i, 0)), pl.BlockSpec((tm, d), lambda i: (i, 0))),
        scratch_shapes=[pltpu.VMEM((tm, w_gmlp), BF16)],
        compiler_params=_params("parallel"),
        name="outproj",
    )(proj, y_tb, attn, h, d_skip, w_glu, b_glu, v_gain, w_s, b_s, w_out, g2)


def _ffn_kernel(x_ref, halo_ref, h_ref, wg_ref, wv_ref, cwg_ref, cwv_ref, cbg_ref, cbv_ref,
                wd_ref, o_ref, sg_ref, sv_ref, *, tm, tiles_per_seq, taps):
    i = pl.program_id(0)

    @pl.when(pl.program_id(1) == 0)
    def _():
        o_ref[...] = h_ref[...]

    halo = halo_ref[...]
    halo = jnp.where(i % tiles_per_seq == 0, jnp.zeros_like(halo), halo)
    x = x_ref[...]

    def conv(w_ref, s_ref, cw_ref, cb_ref):
        s_ref[0:SUBLANES, :] = jnp.dot(halo, w_ref[...], preferred_element_type=F32)
        s_ref[SUBLANES:, :] = jnp.dot(x, w_ref[...], preferred_element_type=F32)
        full = s_ref[...]
        acc = cb_ref[...] + cw_ref[taps - 1:taps, :] * full[SUBLANES:, :]
        for back in range(1, taps):
            shifted = pltpu.roll(full, back, 0)[SUBLANES:, :]
            acc = acc + cw_ref[taps - 1 - back:taps - back, :] * shifted
        return acc

    gate = conv(wg_ref, sg_ref, cwg_ref, cbg_ref)
    half_val = conv(wv_ref, sv_ref, cwv_ref, cbv_ref)
    inner = gate * (GELU_C + (GELU_C * GELU_K) * (gate * gate))
    a = (gate * (1.0 + jnp.tanh(inner)) * half_val).astype(BF16)
    o_ref[...] += jnp.dot(a, wd_ref[...], preferred_element_type=F32)


def _ffn(xn, h, w_up, conv_w, conv_b, w_down, *, layer, seq):
    n, d = h.shape
    f = w_down.shape[1]
    taps = conv_w.shape[0]
    tm, tf = TM_FFN, TF_FFN
    nf = f // tf
    tiles_per_seq = seq // tm
    kern = functools.partial(_ffn_kernel, tm=tm, tiles_per_seq=tiles_per_seq, taps=taps)
    halo_blocks = tm // SUBLANES
    weight_mode = pl.Buffered(1) if nf == 1 else None
    half = jnp.concatenate([jnp.ones((f,), F32), jnp.full((f,), 0.5, F32)])
    conv_w = conv_w * half
    conv_b = conv_b * half
    return pl.pallas_call(
        kern,
        out_shape=jax.ShapeDtypeStruct((n, d), F32),
        grid=(n // tm, nf),
        in_specs=[
            pl.BlockSpec((tm, d), lambda i, j: (i, 0)),
            pl.BlockSpec((SUBLANES, d), lambda i, j: (jnp.maximum(i * halo_blocks - 1, 0), 0)),
            pl.BlockSpec((tm, d), lambda i, j: (i, 0)),
            pl.BlockSpec((None, d, tf), lambda i, j: (layer, 0, j), pipeline_mode=weight_mode),
            pl.BlockSpec((None, d, tf), lambda i, j: (layer, 0, nf + j),
                         pipeline_mode=weight_mode),
            pl.BlockSpec((taps, tf), lambda i, j: (0, j)),
            pl.BlockSpec((taps, tf), lambda i, j: (0, nf + j)),
            pl.BlockSpec((1, tf), lambda i, j: (0, j)),
            pl.BlockSpec((1, tf), lambda i, j: (0, nf + j)),
            pl.BlockSpec((None, tf, d), lambda i, j: (layer, j, 0), pipeline_mode=weight_mode),
        ],
        out_specs=pl.BlockSpec((tm, d), lambda i, j: (i, 0)),
        scratch_shapes=[pltpu.VMEM((tm + SUBLANES, tf), F32), pltpu.VMEM((tm + SUBLANES, tf), F32)],
        compiler_params=_params("parallel", "arbitrary"),
        name="convffn",
    )(xn, xn, h, w_up, w_up, conv_w, conv_w, conv_b, conv_b, w_down)


def _rope_tables(seq, head_dim, reps):
    half = head_dim // 2
    inv = ROPE_THETA ** (-jnp.arange(half, dtype=F32) / half)
    ang = jnp.arange(seq, dtype=F32)[:, None] * inv[None, :]
    ang = jnp.concatenate([ang, ang], axis=-1)
    sign = jnp.concatenate([-jnp.ones((half,), F32), jnp.ones((half,), F32)])
    return jnp.tile(jnp.cos(ang), (1, reps)), jnp.tile(jnp.sin(ang) * sign, (1, reps))


def kernel(x, attn_norm_g, w_in, ssm_a_re, ssm_a_im, ssm_log_dt, ssm_b_re, ssm_b_im, ssm_c_re, ssm_c_im, ssm_d, ssm_w_glu, ssm_b_glu, gmlp_v_g, gmlp_w_s, gmlp_b_s, q_norm_g, k_norm_g, lambda_q1, lambda_k1, lambda_q2, lambda_k2, subln_g, w_out, ffn_norm_g, w_up, conv_w, conv_b, w_down):
    batch, seq, d = x.shape
    depth, _, in_cols = w_in.shape
    w_ssm = ssm_d.shape[1]
    w_gmlp = gmlp_v_g.shape[1]
    w_diff = (in_cols - w_ssm - 2 * w_gmlp) // 3
    qk_dim = q_norm_g.shape[1]
    vdim = subln_g.shape[1]
    heads = w_diff // vdim
    gheads, win = gmlp_w_s.shape[1], gmlp_w_s.shape[2]
    o3 = w_ssm + 2 * w_gmlp
    o4 = o3 + w_diff
    o5 = o4 + w_diff
    assert vdim == LANES and 2 * qk_dim == vdim and batch == SUBLANES

    slab = w_diff
    cos_t, sin_t = _rope_tables(seq, qk_dim, slab // qk_dim)
    lane = jnp.arange(2 * LANES)
    ones = (lane[:, None] // qk_dim == lane[None, :] // qk_dim).astype(BF16)
    q_scale = qk_dim ** -0.5 * math.log2(math.e)
    bt, ct, a_bar = _s5_operators(ssm_a_re, ssm_a_im, ssm_log_dt, ssm_b_re, ssm_b_im,
                                  ssm_c_re, ssm_c_im)
    pos_chunk = jnp.arange(win) // CHUNK
    sgu_mask = pos_chunk[None, :] <= pos_chunk[:, None]
    attn_args = dict(batch=batch, seq=seq, heads=heads, vdim=vdim,
                     qcol=o3 // vdim, kcol=o4 // vdim, vcol=o5 // vdim)

    w_in_bf, w_out_bf = w_in.astype(BF16), w_out.astype(BF16)
    w_up_bf, w_down_bf = w_up.astype(BF16), w_down.astype(BF16)

    h = x.reshape(batch * seq, d)
    for layer in range(depth):
        lambda_init = 0.8 - 0.6 * math.exp(-0.3 * layer)
        lam = (jnp.exp(jnp.sum(lambda_q1[layer] * lambda_k1[layer]))
               - jnp.exp(jnp.sum(lambda_q2[layer] * lambda_k2[layer])) + lambda_init)
        q_gain = q_norm_g[layer] * q_scale
        proj, u_tb = _inproj(
            h, attn_norm_g[layer][None], w_in_bf, cos_t, sin_t,
            jnp.tile(q_gain, slab // qk_dim)[None],
            jnp.tile(k_norm_g[layer], slab // qk_dim)[None], ones,
            layer=layer, batch=batch, seq=seq, o1=w_ssm, o3=o3, o4=o4, o5=o5, head_dim=qk_dim)

        y_tb = _s5(u_tb, bt[layer], ct[layer], a_bar[layer], bsz=batch)

        score_bound = qk_dim * jnp.max(jnp.abs(q_gain)) * jnp.max(jnp.abs(k_norm_g[layer]))
        attn = lax.cond(
            score_bound <= MAX_UNSHIFTED_SCORE,
            functools.partial(_attention, unshifted=True, **attn_args),
            functools.partial(_attention, unshifted=False, **attn_args),
            lam.reshape(1).astype(F32), proj, (subln_g[layer] * (1.0 - lambda_init))[None])

        w_s = jnp.where(sgu_mask[None], gmlp_w_s[layer], 0.0).astype(BF16)
        b_s = jnp.repeat(gmlp_b_s[layer].T, w_gmlp // gheads, axis=1)
        h, xn = _outproj(
            proj, y_tb, attn, h, ssm_d[layer][None], ssm_w_glu[layer].astype(BF16),
            ssm_b_glu[layer][None], gmlp_v_g[layer][None], w_s, b_s,
            w_out_bf, ffn_norm_g[layer][None], layer=layer, seq=seq, w_ssm=w_ssm,
            w_gmlp=w_gmlp)
        h = _ffn(xn, h, w_up_bf, conv_w[layer], conv_b[layer][None], w_down_bf,
                 layer=layer, seq=seq)
    return h.reshape(batch, seq, d)
```

```python
import functools
import math

import jax
import jax.numpy as jnp
from jax import lax
from jax.experimental import pallas as pl
from jax.experimental.pallas import tpu as pltpu

BF16 = jnp.bfloat16
F32 = jnp.float32

EPS = 1e-6
ROPE_THETA = 10000.0
CHUNK = 64
LANES = 128
SUBLANES = 8
MASKED = -1e30
VMEM_LIMIT = 56 * 1024 * 1024
MAX_UNSHIFTED_SCORE = 60.0

TS_SSM = 256
TM_PROJ = 1024
SUB_PROJ = 512
TM_FFN = 1024
TF_FFN = 1408
GELU_C = math.sqrt(2.0 / math.pi)
GELU_K = 0.044715
TQ_ATTN = 512
TK_ATTN = 256
TQ_UNSHIFTED = 1024
TK_UNSHIFTED = 256


def _rms(x, g):
    return x * lax.rsqrt(jnp.mean(x * x, axis=-1, keepdims=True) + EPS) * g


def _params(*sem):
    return pltpu.CompilerParams(dimension_semantics=sem, vmem_limit_bytes=VMEM_LIMIT)


def _inproj_kernel(h_ref, g_ref, w_ref, cos_ref, sin_ref, qg_ref, kg_ref, ones_ref,
                   o_ref, u_ref, *, o1, o3, o4, o5, slab, head_dim, sub):
    lane = lax.broadcasted_iota(jnp.int32, (sub, slab), 1)
    first_half = (lane & (head_dim - 1)) < head_dim // 2
    for r in range(0, h_ref.shape[0], sub):
        rows = slice(r, r + sub)
        xn = _rms(h_ref[rows, :], g_ref[...]).astype(BF16)
        cos = cos_ref[rows, :]
        sin = sin_ref[rows, :]
        for start, stop, gain_ref in ((o3, o4, qg_ref), (o4, o5, kg_ref)):
            for c in range(start, stop, slab):
                y = jnp.dot(xn, w_ref[:, c:c + slab], preferred_element_type=F32)
                y2 = (y * y).astype(BF16)
                piece = ones_ref.shape[0]
                ss = jnp.concatenate(
                    [jnp.dot(y2[:, q:q + piece], ones_ref[...], preferred_element_type=F32)
                     for q in range(0, slab, piece)], axis=1)
                y = y * lax.rsqrt(ss * (1.0 / head_dim) + EPS) * gain_ref[...]
                rot = jnp.where(first_half,
                                pltpu.roll(y, slab - head_dim // 2, 1),
                                pltpu.roll(y, head_dim // 2, 1))
                o_ref[rows, c:c + slab] = (y * cos + rot * sin).astype(BF16)
        front = jnp.dot(xn, w_ref[:, 0:o3], preferred_element_type=F32).astype(BF16)
        o_ref[rows, 0:o3] = front
        u_ref[rows, :] = front[:, 0:o1]
        o_ref[rows, o5:] = jnp.dot(xn, w_ref[:, o5:], preferred_element_type=F32).astype(BF16)


def _inproj(h, g, w, cos_t, sin_t, qg, kg, ones, *, layer, batch, seq, o1, o3, o4, o5, head_dim):
    n, d = h.shape
    cols = w.shape[2]
    tm = TM_PROJ
    slab = cos_t.shape[1]
    tps = seq // tm
    kern = functools.partial(_inproj_kernel, o1=o1, o3=o3, o4=o4, o5=o5, slab=slab,
                             head_dim=head_dim, sub=SUB_PROJ)
    return pl.pallas_call(
        kern,
        out_shape=(jax.ShapeDtypeStruct((n, cols), BF16),
                   jax.ShapeDtypeStruct((seq, batch * o1), BF16)),
        grid=(n // tm,),
        in_specs=[
            pl.BlockSpec((tm, d), lambda i: (i, 0)),
            pl.BlockSpec((1, d), lambda i: (0, 0)),
            pl.BlockSpec((None, d, cols), lambda i: (layer, 0, 0)),
            pl.BlockSpec((tm, slab), lambda i: (i % tps, 0)),
            pl.BlockSpec((tm, slab), lambda i: (i % tps, 0)),
            pl.BlockSpec((1, slab), lambda i: (0, 0)),
            pl.BlockSpec((1, slab), lambda i: (0, 0)),
            pl.BlockSpec(ones.shape, lambda i: (0, 0)),
        ],
        out_specs=(pl.BlockSpec((tm, cols), lambda i: (i, 0)),
                   pl.BlockSpec((tm, o1), lambda i: (i % tps, i // tps))),
        compiler_params=_params("parallel"),
        name="inproj",
    )(h, g, w, cos_t, sin_t, qg, kg, ones)


def _s5_kernel(u_ref, bt_ref, ct_ref, a_ref, y_ref, z_ref, x_ref, ub_ref, yb_ref,
               *, ts, bsz, ns):
    @pl.when(pl.program_id(0) == 0)
    def _():
        x_ref[...] = jnp.zeros(x_ref.shape, F32)

    width = u_ref.shape[1] // bsz
    slabs = width // LANES
    for b in range(bsz):
        ub = u_ref[:, b * width:(b + 1) * width].astype(F32)
        for s in range(slabs):
            ub_ref[s, pl.ds(b, ts, stride=bsz), :] = ub[:, s * LANES:(s + 1) * LANES]
    u = jnp.concatenate([ub_ref[s] for s in range(slabs)], axis=1).astype(BF16)

    half = ts * bsz // 2
    for r in (0, half):
        z_ref[r:r + half, :] = jnp.dot(u[r:r + half, :], bt_ref[...],
                                       preferred_element_type=F32)
    ar = jnp.broadcast_to(a_ref[0:1, :], (bsz, ns))
    ai = jnp.broadcast_to(a_ref[1:2, :], (bsz, ns))

    def body(t, carry):
        xr, xi = carry
        r = pl.multiple_of(t * bsz, bsz)
        nr = ar * xr - ai * xi + z_ref[pl.ds(r, bsz), 0:ns]
        ni = ar * xi + ai * xr + z_ref[pl.ds(r, bsz), ns:]
        z_ref[pl.ds(r, bsz), 0:ns] = nr
        z_ref[pl.ds(r, bsz), ns:] = ni
        return nr, ni

    xr, xi = lax.fori_loop(0, ts, body, (x_ref[:, 0:ns], x_ref[:, ns:]), unroll=8)
    x_ref[:, 0:ns] = xr
    x_ref[:, ns:] = xi
    for r in (0, half):
        yv = jnp.dot(z_ref[r:r + half, :].astype(BF16), ct_ref[...], preferred_element_type=F32)
        for s in range(slabs):
            yb_ref[s, r:r + half, :] = yv[:, s * LANES:(s + 1) * LANES]
    for b in range(bsz):
        for s in range(slabs):
            lanes = slice(b * width + s * LANES, b * width + (s + 1) * LANES)
            y_ref[:, lanes] = yb_ref[s, pl.ds(b, ts, stride=bsz), :].astype(BF16)


def _s5(u_tb, bt, ct, a, *, bsz):
    seq, cols = u_tb.shape
    width = cols // bsz
    ns = a.shape[1]
    ts = TS_SSM
    kern = functools.partial(_s5_kernel, ts=ts, bsz=bsz, ns=ns)
    slab = pltpu.VMEM((width // LANES, ts * bsz, LANES), F32)
    return pl.pallas_call(
        kern,
        out_shape=jax.ShapeDtypeStruct((seq, cols), BF16),
        grid=(seq // ts,),
        in_specs=[
            pl.BlockSpec((ts, cols), lambda i: (i, 0)),
            pl.BlockSpec((width, 2 * ns), lambda i: (0, 0)),
            pl.BlockSpec((2 * ns, width), lambda i: (0, 0)),
            pl.BlockSpec((2, ns), lambda i: (0, 0)),
        ],
        out_specs=pl.BlockSpec((ts, cols), lambda i: (i, 0)),
        scratch_shapes=[pltpu.VMEM((ts * bsz, 2 * ns), F32), pltpu.VMEM((bsz, 2 * ns), F32),
                        slab, slab],
        compiler_params=_params("arbitrary"),
        name="s5",
    )(u_tb, bt, ct, a)


def _s5_operators(a_re, a_im, log_dt, b_re, b_im, c_re, c_im):
    nl, ng, p = a_re.shape
    c = b_re.shape[-1]
    dt = jnp.exp(log_dt)[..., None]
    mag = jnp.exp(a_re * dt)
    ar = mag * jnp.cos(a_im * dt)
    ai = mag * jnp.sin(a_im * dt)
    den = a_re * a_re + a_im * a_im
    kr = ((ar - 1.0) * a_re + ai * a_im) / den
    ki = (ai * a_re - (ar - 1.0) * a_im) / den
    bbr = kr[..., None] * b_re - ki[..., None] * b_im
    bbi = kr[..., None] * b_im + ki[..., None] * b_re
    eye = jnp.eye(ng, dtype=F32)

    def expand_in(b):
        return jnp.einsum("lgpc,gh->lgchp", b, eye).reshape(nl, ng * c, ng * p)

    def expand_out(m):
        return jnp.einsum("lgcp,gh->lgphc", m, eye).reshape(nl, ng * p, ng * c)

    bt = jnp.concatenate([expand_in(bbr), expand_in(bbi)], axis=-1)
    ct = jnp.concatenate([expand_out(c_re), -expand_out(c_im)], axis=1)
    a = jnp.stack([ar.reshape(nl, ng * p), ai.reshape(nl, ng * p)], axis=1)
    return bt.astype(BF16), ct.astype(BF16), a


def _split_maps(q, half):
    lane = lax.broadcasted_iota(jnp.int32, q.shape, 1)
    zero = jnp.zeros_like(q)
    return jnp.where(lane < half, q, zero), jnp.where(lane >= half, q, zero)


def _chunk_mask(rows, cols, r0, c0):
    shift = CHUNK.bit_length() - 1
    row = lax.broadcasted_iota(jnp.int32, (rows, cols), 0) + r0
    col = lax.broadcasted_iota(jnp.int32, (rows, cols), 1) + c0
    return (col >> shift) <= (row >> shift)


def _attn_finish(acc0, l0, acc1, l1, lam, g_ref, o_ref):
    o = acc0 / l0 - lam * (acc1 / l1)
    o_ref[...] = _rms(o, g_ref[...]).astype(BF16)


def _attn_kernel(lam_ref, q_ref, k_ref, v_ref, g_ref, o_ref, m_ref, l_ref, acc_ref,
                 *, tq, tk, half):
    qi = pl.program_id(2)
    qs = _split_maps(q_ref[...], half)
    m_ref[...] = jnp.full(m_ref.shape, MASKED, F32)
    l_ref[...] = jnp.zeros(l_ref.shape, F32)
    acc_ref[...] = jnp.zeros(acc_ref.shape, F32)

    def block(r0, kstart, mask):
        kb = k_ref[pl.ds(kstart, tk), :]
        vb = v_ref[pl.ds(kstart, tk), :]
        for c in range(2):
            s = lax.dot_general(qs[c][r0:], kb, (((1,), (1,)), ((), ())),
                                preferred_element_type=F32)
            if mask is not None:
                s = jnp.where(mask, s, MASKED)
            m_prev = m_ref[c, r0:, :]
            m_new = jnp.maximum(m_prev, jnp.max(s, axis=1, keepdims=True))
            alpha = jnp.exp2(m_prev - m_new)
            p = jnp.exp2(s - jnp.concatenate([m_new] * (tk // LANES), axis=1))
            l_ref[c, r0:, :] = alpha * l_ref[c, r0:, :] + jnp.sum(p, axis=1, keepdims=True)
            acc_ref[c, r0:, :] = alpha * acc_ref[c, r0:, :] + jnp.dot(
                p.astype(BF16), vb, preferred_element_type=F32)
            m_ref[c, r0:, :] = m_new

    def body(j, carry):
        block(0, pl.multiple_of(j * tk, tk), None)
        return carry

    lax.fori_loop(0, qi * (tq // tk), body, 0)
    for jj in range(tq // tk):
        r0 = jj * tk
        block(r0, pl.multiple_of(qi * tq + jj * tk, tk), _chunk_mask(tq - r0, tk, r0, jj * tk))
    _attn_finish(acc_ref[0], l_ref[0], acc_ref[1], l_ref[1], lam_ref[0], g_ref, o_ref)


def _attn_unshifted_kernel(lam_ref, q_ref, k_ref, v_ref, g_ref, o_ref, l_ref, acc_ref,
                           *, tq, tk, half):
    qi = pl.program_id(2)
    qs = _split_maps(q_ref[...], half)

    def clear():
        l_ref[...] = jnp.zeros(l_ref.shape, F32)
        acc_ref[...] = jnp.zeros(acc_ref.shape, F32)

    pl.when((pl.program_id(0) == 0) & (pl.program_id(1) == 0) & (qi == 0))(clear)

    def block(r0, rows, kstart, ksize, mask):
        kb = k_ref[pl.ds(kstart, ksize), :]
        vb = v_ref[pl.ds(kstart, ksize), :]
        for c in range(2):
            s = lax.dot_general(qs[c][r0:r0 + rows], kb, (((1,), (1,)), ((), ())),
                                preferred_element_type=F32)
            p = jnp.exp2(s)
            if mask is not None:
                p = jnp.where(mask, p, 0.0)
            part = p[:, 0:LANES]
            for k in range(1, ksize // LANES):
                part = part + p[:, k * LANES:(k + 1) * LANES]
            l_ref[c, r0:r0 + rows, :] += part
            acc_ref[c, r0:r0 + rows, :] += jnp.dot(p.astype(BF16), vb,
                                                   preferred_element_type=F32)

    def body(j, carry):
        block(0, tq, pl.multiple_of(j * tq, tq), tq, None)
        return carry

    lax.fori_loop(0, qi, body, 0)
    for jj in range(tq // tk):
        r0 = jj * tk
        block(r0, tq - r0, pl.multiple_of(qi * tq + jj * tk, tk), tk,
              _chunk_mask(tq - r0, tk, r0, jj * tk))
    l0 = jnp.sum(l_ref[0], axis=1, keepdims=True)
    l1 = jnp.sum(l_ref[1], axis=1, keepdims=True)
    _attn_finish(acc_ref[0], l0, acc_ref[1], l1, lam_ref[0], g_ref, o_ref)
    clear()


def _attention(lam, proj, sub_g, *, unshifted, batch, seq, heads, vdim, qcol, kcol, vcol):
    tq, tk = (TQ_UNSHIFTED, TK_UNSHIFTED) if unshifted else (TQ_ATTN, TK_ATTN)
    nq = seq // tq
    acc = pltpu.VMEM((2, tq, vdim), F32)
    stat = pltpu.VMEM((2, tq, LANES), F32)
    if unshifted:
        kern = functools.partial(_attn_unshifted_kernel, tq=tq, tk=tk, half=vdim // 2)
        scratch = [stat, acc]
        name = "diffattn_unshifted"
    else:
        kern = functools.partial(_attn_kernel, tq=tq, tk=tk, half=vdim // 2)
        scratch = [stat, stat, acc]
        name = "diffattn"
    return pl.pallas_call(
        kern,
        out_shape=jax.ShapeDtypeStruct((batch * seq, heads * vdim), BF16),
        grid=(batch, heads, nq),
        in_specs=[
            pl.BlockSpec(memory_space=pltpu.SMEM),
            pl.BlockSpec((tq, vdim), lambda b, h, i: (b * nq + i, qcol + h)),
            pl.BlockSpec((seq, vdim), lambda b, h, i: (b, kcol + h)),
            pl.BlockSpec((seq, vdim), lambda b, h, i: (b, vcol + h)),
            pl.BlockSpec((1, vdim), lambda b, h, i: (0, 0)),
        ],
        out_specs=pl.BlockSpec((tq, vdim), lambda b, h, i: (b * nq + i, h)),
        scratch_shapes=scratch,
        compiler_params=_params("arbitrary", "arbitrary", "arbitrary"),
        name=name,
    )(lam, proj, proj, proj, sub_g)


def _outproj_kernel(p_ref, y_ref, a_ref, h_ref, d_ref, wglu_ref, bglu_ref, vgain_ref, ws_ref,
                    bs_ref, wo_ref, g2_ref, ho_ref, xn_ref, sg_ref,
                    *, w_ssm, w_gmlp, win, heads, sub):
    head_shift = (w_gmlp // heads).bit_length() - 1
    head_of_lane = lax.broadcasted_iota(jnp.int32, (win, w_gmlp), 1) >> head_shift
    for r in range(0, h_ref.shape[0], sub):
        rows = slice(r, r + sub)
        p = p_ref[rows, :]
        u = p[:, 0:w_ssm].astype(F32)
        ug = p[:, w_ssm:w_ssm + w_gmlp].astype(F32)
        vg = p[:, w_ssm + w_gmlp:].astype(F32)

        y = jax.nn.gelu(y_ref[rows, :].astype(F32) + d_ref[...] * u)
        z = jnp.dot(y.astype(BF16), wglu_ref[...], preferred_element_type=F32) + bglu_ref[...]
        y_ssm = (y * jax.nn.sigmoid(z)).astype(BF16)

        v = _rms(vg, vgain_ref[...])
        for w in range(sub // win):
            vw = v[w * win:(w + 1) * win, :]
            mixed = bs_ref[...]
            for hh in range(heads):
                vh = jnp.where(head_of_lane == hh, vw, 0.0).astype(BF16)
                mixed = mixed + jnp.dot(ws_ref[hh], vh, preferred_element_type=F32)
            sg_ref[r + w * win:r + (w + 1) * win, :] = (
                ug[w * win:(w + 1) * win, :] * mixed).astype(BF16)

        out = jnp.dot(y_ssm, wo_ref[0:w_ssm, :], preferred_element_type=F32)
        out = out + jnp.dot(sg_ref[rows, :], wo_ref[w_ssm:w_ssm + w_gmlp, :],
                            preferred_element_type=F32)
        out = out + jnp.dot(a_ref[rows, :], wo_ref[w_ssm + w_gmlp:, :],
                            preferred_element_type=F32)
        h_new = h_ref[rows, :] + out
        ho_ref[rows, :] = h_new
        xn_ref[rows, :] = _rms(h_new, g2_ref[...]).astype(BF16)


def _outproj(proj, y_tb, attn, h, d_skip, w_glu, b_glu, v_gain, w_s, b_s, w_out, g2,
             *, layer, seq, w_ssm, w_gmlp):
    n, d = h.shape
    tm = TM_PROJ
    tps = seq // tm
    heads, win, _ = w_s.shape
    w_diff = attn.shape[1]
    front = w_ssm + 2 * w_gmlp
    assert (w_gmlp // heads) & (w_gmlp // heads - 1) == 0
    kern = functools.partial(_outproj_kernel, w_ssm=w_ssm, w_gmlp=w_gmlp, win=win, heads=heads,
                             sub=SUB_PROJ)
    const2 = lambda i: (0, 0)
    return pl.pallas_call(
        kern,
        out_shape=(jax.ShapeDtypeStruct((n, d), F32), jax.ShapeDtypeStruct((n, d), BF16)),
        grid=(n // tm,),
        in_specs=[
            pl.BlockSpec((tm, front), lambda i: (i, 0)),
            pl.BlockSpec((tm, w_ssm), lambda i: (i % tps, i // tps)),
            pl.BlockSpec((tm, w_diff), lambda i: (i, 0)),
            pl.BlockSpec((tm, d), lambda i: (i, 0)),
            pl.BlockSpec((1, w_ssm), const2),
            pl.BlockSpec((w_ssm, w_ssm), const2),
            pl.BlockSpec((1, w_ssm), const2),
            pl.BlockSpec((1, w_gmlp), const2),
            pl.BlockSpec((heads, win, win), lambda i: (0, 0, 0)),
            pl.BlockSpec((win, w_gmlp), const2),
            pl.BlockSpec((None, w_ssm + w_gmlp + w_diff, d), lambda i: (layer, 0, 0)),
            pl.BlockSpec((1, d), const2),
        ],
        out_specs=(pl.BlockSpec((tm, d), lambda i: (i, 0)), pl.BlockSpec((tm, d), lambda i: (i, 0))),
        scratch_shapes=[pltpu.VMEM((tm, w_gmlp), BF16)],
        compiler_params=_params("parallel"),
        name="outproj",
    )(proj, y_tb, attn, h, d_skip, w_glu, b_glu, v_gain, w_s, b_s, w_out, g2)


def _ffn_kernel(x_ref, halo_ref, h_ref, wg_ref, wv_ref, cwg_ref, cwv_ref, cbg_ref, cbv_ref,
                wd_ref, o_ref, sg_ref, sv_ref, *, tm, tiles_per_seq, taps):
    i = pl.program_id(0)

    @pl.when(pl.program_id(1) == 0)
    def _():
        o_ref[...] = h_ref[...]

    halo = halo_ref[...]
    halo = jnp.where(i % tiles_per_seq == 0, jnp.zeros_like(halo), halo)
    x = x_ref[...]

    def conv(w_ref, s_ref, cw_ref, cb_ref):
        s_ref[0:SUBLANES, :] = jnp.dot(halo, w_ref[...], preferred_element_type=F32)
        s_ref[SUBLANES:, :] = jnp.dot(x, w_ref[...], preferred_element_type=F32)
        full = s_ref[...]
        acc = cb_ref[...] + cw_ref[taps - 1:taps, :] * full[SUBLANES:, :]
        for back in range(1, taps):
            shifted = pltpu.roll(full, back, 0)[SUBLANES:, :]
            acc = acc + cw_ref[taps - 1 - back:taps - back, :] * shifted
        return acc

    gate = conv(wg_ref, sg_ref, cwg_ref, cbg_ref)
    half_val = conv(wv_ref, sv_ref, cwv_ref, cbv_ref)
    inner = gate * (GELU_C + (GELU_C * GELU_K) * (gate * gate))
    a = (gate * (1.0 + jnp.tanh(inner)) * half_val).astype(BF16)
    o_ref[...] += jnp.dot(a, wd_ref[...], preferred_element_type=F32)


def _ffn(xn, h, w_up, conv_w, conv_b, w_down, *, layer, seq):
    n, d = h.shape
    f = w_down.shape[1]
    taps = conv_w.shape[0]
    tm, tf = TM_FFN, TF_FFN
    nf = f // tf
    tiles_per_seq = seq // tm
    kern = functools.partial(_ffn_kernel, tm=tm, tiles_per_seq=tiles_per_seq, taps=taps)
    halo_blocks = tm // SUBLANES
    weight_mode = pl.Buffered(1) if nf == 1 else None
    half = jnp.concatenate([jnp.ones((f,), F32), jnp.full((f,), 0.5, F32)])
    conv_w = conv_w * half
    conv_b = conv_b * half
    return pl.pallas_call(
        kern,
        out_shape=jax.ShapeDtypeStruct((n, d), F32),
        grid=(n // tm, nf),
        in_specs=[
            pl.BlockSpec((tm, d), lambda i, j: (i, 0)),
            pl.BlockSpec((SUBLANES, d), lambda i, j: (jnp.maximum(i * halo_blocks - 1, 0), 0)),
            pl.BlockSpec((tm, d), lambda i, j: (i, 0)),
            pl.BlockSpec((None, d, tf), lambda i, j: (layer, 0, j), pipeline_mode=weight_mode),
            pl.BlockSpec((None, d, tf), lambda i, j: (layer, 0, nf + j),
                         pipeline_mode=weight_mode),
            pl.BlockSpec((taps, tf), lambda i, j: (0, j)),
            pl.BlockSpec((taps, tf), lambda i, j: (0, nf + j)),
            pl.BlockSpec((1, tf), lambda i, j: (0, j)),
            pl.BlockSpec((1, tf), lambda i, j: (0, nf + j)),
            pl.BlockSpec((None, tf, d), lambda i, j: (layer, j, 0), pipeline_mode=weight_mode),
        ],
        out_specs=pl.BlockSpec((tm, d), lambda i, j: (i, 0)),
        scratch_shapes=[pltpu.VMEM((tm + SUBLANES, tf), F32), pltpu.VMEM((tm + SUBLANES, tf), F32)],
        compiler_params=_params("parallel", "arbitrary"),
        name="convffn",
    )(xn, xn, h, w_up, w_up, conv_w, conv_w, conv_b, conv_b, w_down)


def _rope_tables(seq, head_dim, reps):
    half = head_dim // 2
    inv = ROPE_THETA ** (-jnp.arange(half, dtype=F32) / half)
    ang = jnp.arange(seq, dtype=F32)[:, None] * inv[None, :]
    ang = jnp.concatenate([ang, ang], axis=-1)
    sign = jnp.concatenate([-jnp.ones((half,), F32), jnp.ones((half,), F32)])
    return jnp.tile(jnp.cos(ang), (1, reps)), jnp.tile(jnp.sin(ang) * sign, (1, reps))


def kernel(x, attn_norm_g, w_in, ssm_a_re, ssm_a_im, ssm_log_dt, ssm_b_re, ssm_b_im, ssm_c_re, ssm_c_im, ssm_d, ssm_w_glu, ssm_b_glu, gmlp_v_g, gmlp_w_s, gmlp_b_s, q_norm_g, k_norm_g, lambda_q1, lambda_k1, lambda_q2, lambda_k2, subln_g, w_out, ffn_norm_g, w_up, conv_w, conv_b, w_down):
    batch, seq, d = x.shape
    depth, _, in_cols = w_in.shape
    w_ssm = ssm_d.shape[1]
    w_gmlp = gmlp_v_g.shape[1]
    w_diff = (in_cols - w_ssm - 2 * w_gmlp) // 3
    qk_dim = q_norm_g.shape[1]
    vdim = subln_g.shape[1]
    heads = w_diff // vdim
    gheads, win = gmlp_w_s.shape[1], gmlp_w_s.shape[2]
    o3 = w_ssm + 2 * w_gmlp
    o4 = o3 + w_diff
    o5 = o4 + w_diff
    assert vdim == LANES and 2 * qk_dim == vdim and batch == SUBLANES

    slab = w_diff
    cos_t, sin_t = _rope_tables(seq, qk_dim, slab // qk_dim)
    lane = jnp.arange(2 * LANES)
    ones = (lane[:, None] // qk_dim == lane[None, :] // qk_dim).astype(BF16)
    q_scale = qk_dim ** -0.5 * math.log2(math.e)
    bt, ct, a_bar = _s5_operators(ssm_a_re, ssm_a_im, ssm_log_dt, ssm_b_re, ssm_b_im,
                                  ssm_c_re, ssm_c_im)
    pos_chunk = jnp.arange(win) // CHUNK
    sgu_mask = pos_chunk[None, :] <= pos_chunk[:, None]
    attn_args = dict(batch=batch, seq=seq, heads=heads, vdim=vdim,
                     qcol=o3 // vdim, kcol=o4 // vdim, vcol=o5 // vdim)

    w_in_bf, w_out_bf = w_in.astype(BF16), w_out.astype(BF16)
    w_up_bf, w_down_bf = w_up.astype(BF16), w_down.astype(BF16)

    h = x.reshape(batch * seq, d)
    for layer in range(depth):
        lambda_init = 0.8 - 0.6 * math.exp(-0.3 * layer)
        lam = (jnp.exp(jnp.sum(lambda_q1[layer] * lambda_k1[layer]))
               - jnp.exp(jnp.sum(lambda_q2[layer] * lambda_k2[layer])) + lambda_init)
        q_gain = q_norm_g[layer] * q_scale
        proj, u_tb = _inproj(
            h, attn_norm_g[layer][None], w_in_bf, cos_t, sin_t,
            jnp.tile(q_gain, slab // qk_dim)[None],
            jnp.tile(k_norm_g[layer], slab // qk_dim)[None], ones,
            layer=layer, batch=batch, seq=seq, o1=w_ssm, o3=o3, o4=o4, o5=o5, head_dim=qk_dim)

        y_tb = _s5(u_tb, bt[layer], ct[layer], a_bar[layer], bsz=batch)

        score_bound = qk_dim * jnp.max(jnp.abs(q_gain)) * jnp.max(jnp.abs(k_norm_g[layer]))
        attn = lax.cond(
            score_bound <= MAX_UNSHIFTED_SCORE,
            functools.partial(_attention, unshifted=True, **attn_args),
            functools.partial(_attention, unshifted=False, **attn_args),
            lam.reshape(1).astype(F32), proj, (subln_g[layer] * (1.0 - lambda_init))[None])

        w_s = jnp.where(sgu_mask[None], gmlp_w_s[layer], 0.0).astype(BF16)
        b_s = jnp.repeat(gmlp_b_s[layer].T, w_gmlp // gheads, axis=1)
        h, xn = _outproj(
            proj, y_tb, attn, h, ssm_d[layer][None], ssm_w_glu[layer].astype(BF16),
            ssm_b_glu[layer][None], gmlp_v_g[layer][None], w_s, b_s,
            w_out_bf, ffn_norm_g[layer][None], layer=layer, seq=seq, w_ssm=w_ssm,
            w_gmlp=w_gmlp)
        h = _ffn(xn, h, w_up_bf, conv_w[layer], conv_b[layer][None], w_down_bf,
                 layer=layer, seq=seq)
    return h.reshape(batch, seq, d)
```

```python
import functools
import math

import jax
import jax.numpy as jnp
from jax import lax
from jax.experimental import pallas as pl
from jax.experimental.pallas import tpu as pltpu

BF16 = jnp.bfloat16
F32 = jnp.float32

EPS = 1e-6
ROPE_THETA = 10000.0
CHUNK = 64
LANES = 128
SUBLANES = 8
MASKED = -1e30
VMEM_LIMIT = 56 * 1024 * 1024
MAX_UNSHIFTED_SCORE = 60.0

TS_SSM = 256
TM_PROJ = 1024
SUB_PROJ = 512
TM_FFN = 1024
TF_FFN = 1408
GELU_C = math.sqrt(2.0 / math.pi)
GELU_K = 0.044715
TQ_ATTN = 512
TK_ATTN = 256
TQ_UNSHIFTED = 1024
TK_UNSHIFTED = 256


def _rms(x, g):
    return x * lax.rsqrt(jnp.mean(x * x, axis=-1, keepdims=True) + EPS) * g


def _params(*sem):
    return pltpu.CompilerParams(dimension_semantics=sem, vmem_limit_bytes=VMEM_LIMIT)


def _inproj_kernel(h_ref, g_ref, w_ref, cos_ref, sin_ref, qg_ref, kg_ref, ones_ref,
                   o_ref, u_ref, *, o1, o3, o4, o5, slab, head_dim, sub):
    lane = lax.broadcasted_iota(jnp.int32, (sub, slab), 1)
    first_half = (lane & (head_dim - 1)) < head_dim // 2
    for r in range(0, h_ref.shape[0], sub):
        rows = slice(r, r + sub)
        xn = _rms(h_ref[rows, :], g_ref[...]).astype(BF16)
        cos = cos_ref[rows, :]
        sin = sin_ref[rows, :]
        for start, stop, gain_ref in ((o3, o4, qg_ref), (o4, o5, kg_ref)):
            for c in range(start, stop, slab):
                y = jnp.dot(xn, w_ref[:, c:c + slab], preferred_element_type=F32)
                y2 = (y * y).astype(BF16)
                piece = ones_ref.shape[0]
                ss = jnp.concatenate(
                    [jnp.dot(y2[:, q:q + piece], ones_ref[...], preferred_element_type=F32)
                     for q in range(0, slab, piece)], axis=1)
                y = y * lax.rsqrt(ss * (1.0 / head_dim) + EPS) * gain_ref[...]
                rot = jnp.where(first_half,
                                pltpu.roll(y, slab - head_dim // 2, 1),
                                pltpu.roll(y, head_dim // 2, 1))
                o_ref[rows, c:c + slab] = (y * cos + rot * sin).astype(BF16)
        front = jnp.dot(xn, w_ref[:, 0:o3], preferred_element_type=F32).astype(BF16)
        o_ref[rows, 0:o3] = front
        u_ref[rows, :] = front[:, 0:o1]
        o_ref[rows, o5:] = jnp.dot(xn, w_ref[:, o5:], preferred_element_type=F32).astype(BF16)


def _inproj(h, g, w, cos_t, sin_t, qg, kg, ones, *, layer, batch, seq, o1, o3, o4, o5, head_dim):
    n, d = h.shape
    cols = w.shape[2]
    tm = TM_PROJ
    slab = cos_t.shape[1]
    tps = seq // tm
    kern = functools.partial(_inproj_kernel, o1=o1, o3=o3, o4=o4, o5=o5, slab=slab,
                             head_dim=head_dim, sub=SUB_PROJ)
    return pl.pallas_call(
        kern,
        out_shape=(jax.ShapeDtypeStruct((n, cols), BF16),
                   jax.ShapeDtypeStruct((seq, batch * o1), BF16)),
        grid=(n // tm,),
        in_specs=[
            pl.BlockSpec((tm, d), lambda i: (i, 0)),
            pl.BlockSpec((1, d), lambda i: (0, 0)),
            pl.BlockSpec((None, d, cols), lambda i: (layer, 0, 0)),
            pl.BlockSpec((tm, slab), lambda i: (i % tps, 0)),
            pl.BlockSpec((tm, slab), lambda i: (i % tps, 0)),
            pl.BlockSpec((1, slab), lambda i: (0, 0)),
            pl.BlockSpec((1, slab), lambda i: (0, 0)),
            pl.BlockSpec(ones.shape, lambda i: (0, 0)),
        ],
        out_specs=(pl.BlockSpec((tm, cols), lambda i: (i, 0)),
                   pl.BlockSpec((tm, o1), lambda i: (i % tps, i // tps))),
        compiler_params=_params("parallel"),
        name="inproj",
    )(h, g, w, cos_t, sin_t, qg, kg, ones)


def _s5_kernel(u_ref, bt_ref, ct_ref, a_ref, y_ref, z_ref, x_ref, ub_ref, yb_ref,
               *, ts, bsz, ns):
    @pl.when(pl.program_id(0) == 0)
    def _():
        x_ref[...] = jnp.zeros(x_ref.shape, F32)

    width = u_ref.shape[1] // bsz
    slabs = width // LANES
    for b in range(bsz):
        ub = u_ref[:, b * width:(b + 1) * width].astype(F32)
        for s in range(slabs):
            ub_ref[s, pl.ds(b, ts, stride=bsz), :] = ub[:, s * LANES:(s + 1) * LANES]
    u = jnp.concatenate([ub_ref[s] for s in range(slabs)], axis=1).astype(BF16)

    half = ts * bsz // 2
    for r in (0, half):
        z_ref[r:r + half, :] = jnp.dot(u[r:r + half, :], bt_ref[...],
                                       preferred_element_type=F32)
    ar = jnp.broadcast_to(a_ref[0:1, :], (bsz, ns))
    ai = jnp.broadcast_to(a_ref[1:2, :], (bsz, ns))

    def body(t, carry):
        xr, xi = carry
        r = pl.multiple_of(t * bsz, bsz)
        nr = ar * xr - ai * xi + z_ref[pl.ds(r, bsz), 0:ns]
        ni = ar * xi + ai * xr + z_ref[pl.ds(r, bsz), ns:]
        z_ref[pl.ds(r, bsz), 0:ns] = nr
        z_ref[pl.ds(r, bsz), ns:] = ni
        return nr, ni

    xr, xi = lax.fori_loop(0, ts, body, (x_ref[:, 0:ns], x_ref[:, ns:]), unroll=8)
    x_ref[:, 0:ns] = xr
    x_ref[:, ns:] = xi
    for r in (0, half):
        yv = jnp.dot(z_ref[r:r + half, :].astype(BF16), ct_ref[...], preferred_element_type=F32)
        for s in range(slabs):
            yb_ref[s, r:r + half, :] = yv[:, s * LANES:(s + 1) * LANES]
    for b in range(bsz):
        for s in range(slabs):
            lanes = slice(b * width + s * LANES, b * width + (s + 1) * LANES)
            y_ref[:, lanes] = yb_ref[s, pl.ds(b, ts, stride=bsz), :].astype(BF16)


def _s5(u_tb, bt, ct, a, *, bsz):
    seq, cols = u_tb.shape
    width = cols // bsz
    ns = a.shape[1]
    ts = TS_SSM
    kern = functools.partial(_s5_kernel, ts=ts, bsz=bsz, ns=ns)
    slab = pltpu.VMEM((width // LANES, ts * bsz, LANES), F32)
    return pl.pallas_call(
        kern,
        out_shape=jax.ShapeDtypeStruct((seq, cols), BF16),
        grid=(seq // ts,),
        in_specs=[
            pl.BlockSpec((ts, cols), lambda i: (i, 0)),
            pl.BlockSpec((width, 2 * ns), lambda i: (0, 0)),
            pl.BlockSpec((2 * ns, width), lambda i: (0, 0)),
            pl.BlockSpec((2, ns), lambda i: (0, 0)),
        ],
        out_specs=pl.BlockSpec((ts, cols), lambda i: (i, 0)),
        scratch_shapes=[pltpu.VMEM((ts * bsz, 2 * ns), F32), pltpu.VMEM((bsz, 2 * ns), F32),
                        slab, slab],
        compiler_params=_params("arbitrary"),
        name="s5",
    )(u_tb, bt, ct, a)


def _s5_operators(a_re, a_im, log_dt, b_re, b_im, c_re, c_im):
    nl, ng, p = a_re.shape
    c = b_re.shape[-1]
    dt = jnp.exp(log_dt)[..., None]
    mag = jnp.exp(a_re * dt)
    ar = mag * jnp.cos(a_im * dt)
    ai = mag * jnp.sin(a_im * dt)
    den = a_re * a_re + a_im * a_im
    kr = ((ar - 1.0) * a_re + ai * a_im) / den
    ki = (ai * a_re - (ar - 1.0) * a_im) / den
    bbr = kr[..., None] * b_re - ki[..., None] * b_im
    bbi = kr[..., None] * b_im + ki[..., None] * b_re
    eye = jnp.eye(ng, dtype=F32)

    def expand_in(b):
        return jnp.einsum("lgpc,gh->lgchp", b, eye).reshape(nl, ng * c, ng * p)

    def expand_out(m):
        return jnp.einsum("lgcp,gh->lgphc", m, eye).reshape(nl, ng * p, ng * c)

    bt = jnp.concatenate([expand_in(bbr), expand_in(bbi)], axis=-1)
    ct = jnp.concatenate([expand_out(c_re), -expand_out(c_im)], axis=1)
    a = jnp.stack([ar.reshape(nl, ng * p), ai.reshape(nl, ng * p)], axis=1)
    return bt.astype(BF16), ct.astype(BF16), a


def _split_maps(q, half):
    lane = lax.broadcasted_iota(jnp.int32, q.shape, 1)
    zero = jnp.zeros_like(q)
    return jnp.where(lane < half, q, zero), jnp.where(lane >= half, q, zero)


def _chunk_mask(rows, cols, r0, c0):
    shift = CHUNK.bit_length() - 1
    row = lax.broadcasted_iota(jnp.int32, (rows, cols), 0) + r0
    col = lax.broadcasted_iota(jnp.int32, (rows, cols), 1) + c0
    return (col >> shift) <= (row >> shift)


def _attn_finish(acc0, l0, acc1, l1, lam, o_ref):
    o_ref[...] = (acc0 / l0 - lam * (acc1 / l1)).astype(BF16)


def _attn_kernel(lam_ref, q_ref, k_ref, v_ref, o_ref, m_ref, l_ref, acc_ref,
                 *, tq, tk, half):
    qi = pl.program_id(2)
    qs = _split_maps(q_ref[...], half)
    m_ref[...] = jnp.full(m_ref.shape, MASKED, F32)
    l_ref[...] = jnp.zeros(l_ref.shape, F32)
    acc_ref[...] = jnp.zeros(acc_ref.shape, F32)

    def block(r0, kstart, mask):
        kb = k_ref[pl.ds(kstart, tk), :]
        vb = v_ref[pl.ds(kstart, tk), :]
        for c in range(2):
            s = lax.dot_general(qs[c][r0:], kb, (((1,), (1,)), ((), ())),
                                preferred_element_type=F32)
            if mask is not None:
                s = jnp.where(mask, s, MASKED)
            m_prev = m_ref[c, r0:, :]
            m_new = jnp.maximum(m_prev, jnp.max(s, axis=1, keepdims=True))
            alpha = jnp.exp2(m_prev - m_new)
            p = jnp.exp2(s - jnp.concatenate([m_new] * (tk // LANES), axis=1))
            l_ref[c, r0:, :] = alpha * l_ref[c, r0:, :] + jnp.sum(p, axis=1, keepdims=True)
            acc_ref[c, r0:, :] = alpha * acc_ref[c, r0:, :] + jnp.dot(
                p.astype(BF16), vb, preferred_element_type=F32)
            m_ref[c, r0:, :] = m_new

    def body(j, carry):
        block(0, pl.multiple_of(j * tk, tk), None)
        return carry

    lax.fori_loop(0, qi * (tq // tk), body, 0)
    for jj in range(tq // tk):
        r0 = jj * tk
        block(r0, pl.multiple_of(qi * tq + jj * tk, tk), _chunk_mask(tq - r0, tk, r0, jj * tk))
    _attn_finish(acc_ref[0], l_ref[0], acc_ref[1], l_ref[1], lam_ref[0], o_ref)


def _attn_unshifted_kernel(lam_ref, q_ref, k_ref, v_ref, o_ref, l_ref, acc_ref,
                           *, tq, tk, half):
    qi = pl.program_id(2)
    qs = _split_maps(q_ref[...], half)

    def clear():
        l_ref[...] = jnp.zeros(l_ref.shape, F32)
        acc_ref[...] = jnp.zeros(acc_ref.shape, F32)

    pl.when((pl.program_id(0) == 0) & (pl.program_id(1) == 0) & (qi == 0))(clear)

    def block(r0, rows, kstart, ksize, mask):
        kb = k_ref[pl.ds(kstart, ksize), :]
        vb = v_ref[pl.ds(kstart, ksize), :]
        for c in range(2):
            s = lax.dot_general(qs[c][r0:r0 + rows], kb, (((1,), (1,)), ((), ())),
                                preferred_element_type=F32)
            p = jnp.exp2(s)
            if mask is not None:
                p = jnp.where(mask, p, 0.0)
            part = p[:, 0:LANES]
            for k in range(1, ksize // LANES):
                part = part + p[:, k * LANES:(k + 1) * LANES]
            l_ref[c, r0:r0 + rows, :] += part
            acc_ref[c, r0:r0 + rows, :] += jnp.dot(p.astype(BF16), vb,
                                                   preferred_element_type=F32)

    def body(j, carry):
        block(0, tq, pl.multiple_of(j * tq, tq), tq, None)
        return carry

    lax.fori_loop(0, qi, body, 0)
    for jj in range(tq // tk):
        r0 = jj * tk
        block(r0, tq - r0, pl.multiple_of(qi * tq + jj * tk, tk), tk,
              _chunk_mask(tq - r0, tk, r0, jj * tk))
    l0 = jnp.sum(l_ref[0], axis=1, keepdims=True)
    l1 = jnp.sum(l_ref[1], axis=1, keepdims=True)
    _attn_finish(acc_ref[0], l0, acc_ref[1], l1, lam_ref[0], o_ref)
    clear()


def _attention(lam, proj, *, unshifted, batch, seq, heads, vdim, qcol, kcol, vcol):
    tq, tk = (TQ_UNSHIFTED, TK_UNSHIFTED) if unshifted else (TQ_ATTN, TK_ATTN)
    nq = seq // tq
    acc = pltpu.VMEM((2, tq, vdim), F32)
    stat = pltpu.VMEM((2, tq, LANES), F32)
    if unshifted:
        kern = functools.partial(_attn_unshifted_kernel, tq=tq, tk=tk, half=vdim // 2)
        scratch = [stat, acc]
        name = "diffattn_unshifted"
    else:
        kern = functools.partial(_attn_kernel, tq=tq, tk=tk, half=vdim // 2)
        scratch = [stat, stat, acc]
        name = "diffattn"
    return pl.pallas_call(
        kern,
        out_shape=jax.ShapeDtypeStruct((batch * seq, heads * vdim), BF16),
        grid=(batch, heads, nq),
        in_specs=[
            pl.BlockSpec(memory_space=pltpu.SMEM),
            pl.BlockSpec((tq, vdim), lambda b, h, i: (b * nq + i, qcol + h)),
            pl.BlockSpec((seq, vdim), lambda b, h, i: (b, kcol + h)),
            pl.BlockSpec((seq, vdim), lambda b, h, i: (b, vcol + h)),
        ],
        out_specs=pl.BlockSpec((tq, vdim), lambda b, h, i: (b * nq + i, h)),
        scratch_shapes=scratch,
        compiler_params=_params("arbitrary", "arbitrary", "arbitrary"),
        name=name,
    )(lam, proj, proj, proj)


def _outproj_kernel(p_ref, y_ref, a_ref, h_ref, d_ref, wglu_ref, bglu_ref, vgain_ref, ws_ref,
                    bs_ref, wo_ref, g2_ref, sa_ref, ho_ref, xn_ref, sg_ref,
                    *, w_ssm, w_gmlp, win, heads, sub):
    vdim = sa_ref.shape[1]
    head_shift = (w_gmlp // heads).bit_length() - 1
    head_of_lane = lax.broadcasted_iota(jnp.int32, (win, w_gmlp), 1) >> head_shift
    for r in range(0, h_ref.shape[0], sub):
        rows = slice(r, r + sub)
        p = p_ref[rows, :]
        u = p[:, 0:w_ssm].astype(F32)
        ug = p[:, w_ssm:w_ssm + w_gmlp].astype(F32)
        vg = p[:, w_ssm + w_gmlp:].astype(F32)

        y = jax.nn.gelu(y_ref[rows, :].astype(F32) + d_ref[...] * u)
        z = jnp.dot(y.astype(BF16), wglu_ref[...], preferred_element_type=F32) + bglu_ref[...]
        y_ssm = (y * jax.nn.sigmoid(z)).astype(BF16)

        v = _rms(vg, vgain_ref[...])
        for w in range(sub // win):
            vw = v[w * win:(w + 1) * win, :]
            mixed = bs_ref[...]
            for hh in range(heads):
                vh = jnp.where(head_of_lane == hh, vw, 0.0).astype(BF16)
                mixed = mixed + jnp.dot(ws_ref[hh], vh, preferred_element_type=F32)
            sg_ref[r + w * win:r + (w + 1) * win, :] = (
                ug[w * win:(w + 1) * win, :] * mixed).astype(BF16)

        out = jnp.dot(y_ssm, wo_ref[0:w_ssm, :], preferred_element_type=F32)
        out = out + jnp.dot(sg_ref[rows, :], wo_ref[w_ssm:w_ssm + w_gmlp, :],
                            preferred_element_type=F32)
        a = a_ref[rows, :].astype(F32)
        a = jnp.concatenate([_rms(a[:, c:c + vdim], sa_ref[...])
                             for c in range(0, a.shape[1], vdim)], axis=1).astype(BF16)
        out = out + jnp.dot(a, wo_ref[w_ssm + w_gmlp:, :], preferred_element_type=F32)
        h_new = h_ref[rows, :] + out
        ho_ref[rows, :] = h_new
        xn_ref[rows, :] = _rms(h_new, g2_ref[...]).astype(BF16)


def _outproj(proj, y_tb, attn, h, d_skip, w_glu, b_glu, v_gain, w_s, b_s, w_out, g2, sub_g,
             *, layer, seq, w_ssm, w_gmlp):
    n, d = h.shape
    tm = TM_PROJ
    tps = seq // tm
    heads, win, _ = w_s.shape
    w_diff = attn.shape[1]
    front = w_ssm + 2 * w_gmlp
    assert (w_gmlp // heads) & (w_gmlp // heads - 1) == 0
    kern = functools.partial(_outproj_kernel, w_ssm=w_ssm, w_gmlp=w_gmlp, win=win, heads=heads,
                             sub=SUB_PROJ)
    const2 = lambda i: (0, 0)
    return pl.pallas_call(
        kern,
        out_shape=(jax.ShapeDtypeStruct((n, d), F32), jax.ShapeDtypeStruct((n, d), BF16)),
        grid=(n // tm,),
        in_specs=[
            pl.BlockSpec((tm, front), lambda i: (i, 0)),
            pl.BlockSpec((tm, w_ssm), lambda i: (i % tps, i // tps)),
            pl.BlockSpec((tm, w_diff), lambda i: (i, 0)),
            pl.BlockSpec((tm, d), lambda i: (i, 0)),
            pl.BlockSpec((1, w_ssm), const2),
            pl.BlockSpec((w_ssm, w_ssm), const2),
            pl.BlockSpec((1, w_ssm), const2),
            pl.BlockSpec((1, w_gmlp), const2),
            pl.BlockSpec((heads, win, win), lambda i: (0, 0, 0)),
            pl.BlockSpec((win, w_gmlp), const2),
            pl.BlockSpec((None, w_ssm + w_gmlp + w_diff, d), lambda i: (layer, 0, 0)),
            pl.BlockSpec((1, d), const2),
            pl.BlockSpec(sub_g.shape, const2),
        ],
        out_specs=(pl.BlockSpec((tm, d), lambda i: (i, 0)), pl.BlockSpec((tm, d), lambda i: (i, 0))),
        scratch_shapes=[pltpu.VMEM((tm, w_gmlp), BF16)],
        compiler_params=_params("parallel"),
        name="outproj",
    )(proj, y_tb, attn, h, d_skip, w_glu, b_glu, v_gain, w_s, b_s, w_out, g2, sub_g)


def _ffn_kernel(x_ref, halo_ref, h_ref, wg_ref, wv_ref, cwg_ref, cwv_ref, cbg_ref, cbv_ref,
                wd_ref, o_ref, sg_ref, sv_ref, *, tm, tiles_per_seq, taps):
    i = pl.program_id(0)

    @pl.when(pl.program_id(1) == 0)
    def _():
        o_ref[...] = h_ref[...]

    halo = halo_ref[...]
    halo = jnp.where(i % tiles_per_seq == 0, jnp.zeros_like(halo), halo)
    x = x_ref[...]

    def conv(w_ref, s_ref, cw_ref, cb_ref):
        s_ref[0:SUBLANES, :] = jnp.dot(halo, w_ref[...], preferred_element_type=F32)
        s_ref[SUBLANES:, :] = jnp.dot(x, w_ref[...], preferred_element_type=F32)
        full = s_ref[...]
        acc = cb_ref[...] + cw_ref[taps - 1:taps, :] * full[SUBLANES:, :]
        for back in range(1, taps):
            shifted = pltpu.roll(full, back, 0)[SUBLANES:, :]
            acc = acc + cw_ref[taps - 1 - back:taps - back, :] * shifted
        return acc

    gate = conv(wg_ref, sg_ref, cwg_ref, cbg_ref)
    half_val = conv(wv_ref, sv_ref, cwv_ref, cbv_ref)
    inner = gate * (GELU_C + (GELU_C * GELU_K) * (gate * gate))
    a = (gate * (1.0 + jnp.tanh(inner)) * half_val).astype(BF16)
    o_ref[...] += jnp.dot(a, wd_ref[...], preferred_element_type=F32)


def _ffn(xn, h, w_up, conv_w, conv_b, w_down, *, layer, seq):
    n, d = h.shape
    f = w_down.shape[1]
    taps = conv_w.shape[0]
    tm, tf = TM_FFN, TF_FFN
    nf = f // tf
    tiles_per_seq = seq // tm
    kern = functools.partial(_ffn_kernel, tm=tm, tiles_per_seq=tiles_per_seq, taps=taps)
    halo_blocks = tm // SUBLANES
    weight_mode = pl.Buffered(1) if nf == 1 else None
    half = jnp.concatenate([jnp.ones((f,), F32), jnp.full((f,), 0.5, F32)])
    conv_w = conv_w * half
    conv_b = conv_b * half
    return pl.pallas_call(
        kern,
        out_shape=jax.ShapeDtypeStruct((n, d), F32),
        grid=(n // tm, nf),
        in_specs=[
            pl.BlockSpec((tm, d), lambda i, j: (i, 0)),
            pl.BlockSpec((SUBLANES, d), lambda i, j: (jnp.maximum(i * halo_blocks - 1, 0), 0)),
            pl.BlockSpec((tm, d), lambda i, j: (i, 0)),
            pl.BlockSpec((None, d, tf), lambda i, j: (layer, 0, j), pipeline_mode=weight_mode),
            pl.BlockSpec((None, d, tf), lambda i, j: (layer, 0, nf + j),
                         pipeline_mode=weight_mode),
            pl.BlockSpec((taps, tf), lambda i, j: (0, j)),
            pl.BlockSpec((taps, tf), lambda i, j: (0, nf + j)),
            pl.BlockSpec((1, tf), lambda i, j: (0, j)),
            pl.BlockSpec((1, tf), lambda i, j: (0, nf + j)),
            pl.BlockSpec((None, tf, d), lambda i, j: (layer, j, 0), pipeline_mode=weight_mode),
        ],
        out_specs=pl.BlockSpec((tm, d), lambda i, j: (i, 0)),
        scratch_shapes=[pltpu.VMEM((tm + SUBLANES, tf), F32), pltpu.VMEM((tm + SUBLANES, tf), F32)],
        compiler_params=_params("parallel", "arbitrary"),
        name="convffn",
    )(xn, xn, h, w_up, w_up, conv_w, conv_w, conv_b, conv_b, w_down)


def _rope_tables(seq, head_dim, reps):
    half = head_dim // 2
    inv = ROPE_THETA ** (-jnp.arange(half, dtype=F32) / half)
    ang = jnp.arange(seq, dtype=F32)[:, None] * inv[None, :]
    ang = jnp.concatenate([ang, ang], axis=-1)
    sign = jnp.concatenate([-jnp.ones((half,), F32), jnp.ones((half,), F32)])
    return jnp.tile(jnp.cos(ang), (1, reps)), jnp.tile(jnp.sin(ang) * sign, (1, reps))


def kernel(x, attn_norm_g, w_in, ssm_a_re, ssm_a_im, ssm_log_dt, ssm_b_re, ssm_b_im, ssm_c_re, ssm_c_im, ssm_d, ssm_w_glu, ssm_b_glu, gmlp_v_g, gmlp_w_s, gmlp_b_s, q_norm_g, k_norm_g, lambda_q1, lambda_k1, lambda_q2, lambda_k2, subln_g, w_out, ffn_norm_g, w_up, conv_w, conv_b, w_down):
    batch, seq, d = x.shape
    depth, _, in_cols = w_in.shape
    w_ssm = ssm_d.shape[1]
    w_gmlp = gmlp_v_g.shape[1]
    w_diff = (in_cols - w_ssm - 2 * w_gmlp) // 3
    qk_dim = q_norm_g.shape[1]
    vdim = subln_g.shape[1]
    heads = w_diff // vdim
    gheads, win = gmlp_w_s.shape[1], gmlp_w_s.shape[2]
    o3 = w_ssm + 2 * w_gmlp
    o4 = o3 + w_diff
    o5 = o4 + w_diff
    assert vdim == LANES and 2 * qk_dim == vdim and batch == SUBLANES

    slab = w_diff
    cos_t, sin_t = _rope_tables(seq, qk_dim, slab // qk_dim)
    lane = jnp.arange(2 * LANES)
    ones = (lane[:, None] // qk_dim == lane[None, :] // qk_dim).astype(BF16)
    q_scale = qk_dim ** -0.5 * math.log2(math.e)
    bt, ct, a_bar = _s5_operators(ssm_a_re, ssm_a_im, ssm_log_dt, ssm_b_re, ssm_b_im,
                                  ssm_c_re, ssm_c_im)
    pos_chunk = jnp.arange(win) // CHUNK
    sgu_mask = pos_chunk[None, :] <= pos_chunk[:, None]
    attn_args = dict(batch=batch, seq=seq, heads=heads, vdim=vdim,
                     qcol=o3 // vdim, kcol=o4 // vdim, vcol=o5 // vdim)

    w_in_bf, w_out_bf = w_in.astype(BF16), w_out.astype(BF16)
    w_up_bf, w_down_bf = w_up.astype(BF16), w_down.astype(BF16)

    h = x.reshape(batch * seq, d)
    for layer in range(depth):
        lambda_init = 0.8 - 0.6 * math.exp(-0.3 * layer)
        lam = (jnp.exp(jnp.sum(lambda_q1[layer] * lambda_k1[layer]))
               - jnp.exp(jnp.sum(lambda_q2[layer] * lambda_k2[layer])) + lambda_init)
        q_gain = q_norm_g[layer] * q_scale
        proj, u_tb = _inproj(
            h, attn_norm_g[layer][None], w_in_bf, cos_t, sin_t,
            jnp.tile(q_gain, slab // qk_dim)[None],
            jnp.tile(k_norm_g[layer], slab // qk_dim)[None], ones,
            layer=layer, batch=batch, seq=seq, o1=w_ssm, o3=o3, o4=o4, o5=o5, head_dim=qk_dim)

        y_tb = _s5(u_tb, bt[layer], ct[layer], a_bar[layer], bsz=batch)

        score_bound = qk_dim * jnp.max(jnp.abs(q_gain)) * jnp.max(jnp.abs(k_norm_g[layer]))
        attn = lax.cond(
            score_bound <= MAX_UNSHIFTED_SCORE,
            functools.partial(_attention, unshifted=True, **attn_args),
            functools.partial(_attention, unshifted=False, **attn_args),
            lam.reshape(1).astype(F32), proj)

        w_s = jnp.where(sgu_mask[None], gmlp_w_s[layer], 0.0).astype(BF16)
        b_s = jnp.repeat(gmlp_b_s[layer].T, w_gmlp // gheads, axis=1)
        h, xn = _outproj(
            proj, y_tb, attn, h, ssm_d[layer][None], ssm_w_glu[layer].astype(BF16),
            ssm_b_glu[layer][None], gmlp_v_g[layer][None], w_s, b_s,
            w_out_bf, ffn_norm_g[layer][None], (subln_g[layer] * (1.0 - lambda_init))[None],
            layer=layer, seq=seq, w_ssm=w_ssm, w_gmlp=w_gmlp)
        h = _ffn(xn, h, w_up_bf, conv_w[layer], conv_b[layer][None], w_down_bf,
                 layer=layer, seq=seq)
    return h.reshape(batch, seq, d)
```
